```python
import math
import jax, jax.numpy as jnp
from jax import lax
import numpy as np

D_MODEL = 2048
BATCH = 2
SEQ = 8192
DEPTH = 4
DEC_BATCH = 8
DEC_SEQ = 32
PAST_LEN = 2048

CHUNK = 64
RMS_EPS = 1e-6
NEG_INF = -1e30
POOL_WIDTH = D_MODEL // 4
POOL_WINDOWS = (2, 4, 8, 16)
POOL_GROUPS = len(POOL_WINDOWS)
POOL_GW = POOL_WIDTH // POOL_GROUPS
POOL_HIST = max(POOL_WINDOWS) - 1
ATT_HEADS = 8
QK_DIM = 64
V_DIM = 2 * QK_DIM
ATT_WIDTH = ATT_HEADS * V_DIM
ROT_DIM = QK_DIM // 4
ROPE_THETA = 500000.0
Q_BLOCK = 128
SSM_WIDTH = D_MODEL // 4
SSM_GROUP = 16
SSM_GROUPS = SSM_WIDTH // SSM_GROUP
SSM_STATE = 64
IN_WIDTH = POOL_WIDTH + 3 * ATT_WIDTH + SSM_WIDTH
IN_SPLITS = [POOL_WIDTH, POOL_WIDTH + ATT_WIDTH, POOL_WIDTH + 2 * ATT_WIDTH, POOL_WIDTH + 3 * ATT_WIDTH]
N_BRANCH = 3
N_EXPERTS = 32
TOP_K = 4
D_FF = D_MODEL
SWIGLU_LIMIT = 7.0
SWIGLU_ALPHA = 1.702
MOE_BLOCK = 128

kernel_name = 'hybrid_pool_diffattn_s5_moe_stream_step'


def rms_norm(x, g):
    xf = x.astype(jnp.float32)
    y = xf * lax.rsqrt(jnp.mean(xf * xf, axis=-1, keepdims=True) + RMS_EPS)
    return (y * g.astype(jnp.float32)).astype(x.dtype)


def ada_rms_norm(x, g, shift, scale):
    return rms_norm(x, g) * (1 + scale[:, None, :]) + shift[:, None, :]


def rotary(x, pos):
    half = ROT_DIM // 2
    inv_freq = jnp.exp(-math.log(ROPE_THETA) * jnp.arange(half, dtype=jnp.float32) * (2.0 / ROT_DIM))
    ang = pos.astype(jnp.float32)[:, None] * inv_freq[None, :]
    cos = jnp.cos(ang)[:, None, None, :]
    sin = jnp.sin(ang)[:, None, None, :]
    xf = x.astype(jnp.float32)
    x1, x2, rest = xf[..., :half], xf[..., half:ROT_DIM], xf[..., ROT_DIM:]
    out = jnp.concatenate([x1 * cos - x2 * sin, x2 * cos + x1 * sin, rest], axis=-1)
    return out.astype(x.dtype)


def pool_mixer(u, prev, pos, w_mix, scale):
    L = u.shape[1]
    ext = jnp.concatenate([prev.astype(jnp.float32), u.astype(jnp.float32)], axis=1)
    cs = jnp.pad(jnp.cumsum(ext, axis=1), ((0, 0), (1, 0), (0, 0)))
    end = cs[:, POOL_HIST + 1:POOL_HIST + 1 + L]
    uf = u.astype(jnp.float32)
    outs = []
    for g, w in enumerate(POOL_WINDOWS):
        sl = slice(g * POOL_GW, (g + 1) * POOL_GW)
        win_sum = end[..., sl] - cs[:, POOL_HIST + 1 - w:POOL_HIST + 1 - w + L, sl]
        cnt = jnp.minimum(pos + 1, w).astype(jnp.float32)[None, :, None]
        d = win_sum / cnt - uf[..., sl]
        outs.append(jnp.einsum('blc,cd->bld', d.astype(u.dtype), w_mix[g]))
    y = jnp.concatenate(outs, axis=-1) * scale
    return y, ext[:, -POOL_HIST:].astype(u.dtype)


def diff_attend(q, k, v, mask, lam):
    s = jnp.einsum('bqhmd,bkhmd->bhmqk', q.astype(jnp.float32), k.astype(jnp.float32)) * (QK_DIM ** -0.5)
    if mask is not None:
        s = jnp.where(mask, s, NEG_INF)
    pr = jax.nn.softmax(s, axis=-1)
    a = pr[:, :, 0] - lam * pr[:, :, 1]
    return jnp.einsum('bhqk,bkhd->bqhd', a, v.astype(jnp.float32))


def diff_attn_prompt(q, k, v, lam):
    Bn, L = q.shape[0], q.shape[1]
    nb = L // Q_BLOCK
    qb = q.reshape(Bn, nb, Q_BLOCK, ATT_HEADS, 2, QK_DIM).swapaxes(0, 1)
    kchunk = jnp.arange(L) // CHUNK

    def block(args):
        qi, bi = args
        qchunk = (bi * Q_BLOCK + jnp.arange(Q_BLOCK)) // CHUNK
        mask = kchunk[None, :] <= qchunk[:, None]
        return diff_attend(qi, k, v, mask, lam)

    o = lax.map(block, (qb, jnp.arange(nb)))
    return o.swapaxes(0, 1).reshape(Bn, L, ATT_HEADS, V_DIM)


def _ssm_combine(e1, e2):
    a1, b1 = e1
    a2, b2 = e2
    return a1 * a2, a2 * b1 + b2


def s5_mixer(u, h0_re, h0_im, lam_re, lam_im, log_dt, b_re, b_im, c_re, c_im, d_skip, w_glu, b_glu):
    Bn, L = u.shape[0], u.shape[1]
    uf = u.astype(jnp.float32)
    ug = uf.reshape(Bn, L, SSM_GROUPS, SSM_GROUP).astype(jnp.complex64)
    lam = lax.complex(lam_re.astype(jnp.float32), lam_im.astype(jnp.float32))
    dt = jnp.exp(log_dt.astype(jnp.float32))[:, None]
    lam_bar = jnp.exp(lam * dt)
    b_bar = ((lam_bar - 1.0) / lam)[..., None] * lax.complex(b_re.astype(jnp.float32), b_im.astype(jnp.float32))
    bu = jnp.einsum('gnc,blgc->blgn', b_bar, ug)
    a = jnp.broadcast_to(lam_bar, bu.shape)
    a_cum, xs = lax.associative_scan(_ssm_combine, (a, bu), axis=1)
    xs = xs + a_cum * lax.complex(h0_re.astype(jnp.float32), h0_im.astype(jnp.float32))[:, None]
    cmat = lax.complex(c_re.astype(jnp.float32), c_im.astype(jnp.float32))
    y = jnp.real(jnp.einsum('gcn,blgn->blgc', cmat, xs)).reshape(Bn, L, SSM_WIDTH) + d_skip.astype(jnp.float32) * uf
    g = jax.nn.gelu(y)
    out = g * jax.nn.sigmoid(g @ w_glu.astype(jnp.float32) + b_glu.astype(jnp.float32))
    last = xs[:, -1]
    return out.astype(u.dtype), jnp.real(last), jnp.imag(last)


def moe_ffn(x, w_router, b_router, w_up, b_up, w_down, b_down):
    shp = x.shape
    xt = x.reshape(-1, shp[-1])
    T = xt.shape[0]
    logits = (xt @ w_router).astype(jnp.float32) + b_router.astype(jnp.float32)
    top_v, top_i = lax.top_k(logits, TOP_K)
    gates = jax.nn.softmax(top_v, axis=-1)
    TK = T * TOP_K
    flat_e = top_i.reshape(TK)
    flat_t = jnp.repeat(jnp.arange(T, dtype=jnp.int32), TOP_K)
    flat_g = gates.reshape(TK)
    order = jnp.argsort(flat_e)
    se = flat_e[order]
    counts = jnp.bincount(flat_e, length=N_EXPERTS)
    padded = (counts + MOE_BLOCK - 1) // MOE_BLOCK * MOE_BLOCK
    pad_end = jnp.cumsum(padded)
    pad_start = pad_end - padded
    start = jnp.cumsum(counts) - counts
    dest = pad_start[se] + jnp.arange(TK) - start[se]
    n_blocks = -(-TK // MOE_BLOCK) + N_EXPERTS
    slot_tok = jnp.full((n_blocks * MOE_BLOCK,), T, jnp.int32).at[dest].set(flat_t[order])
    slot_gate = jnp.zeros((n_blocks * MOE_BLOCK,), jnp.float32).at[dest].set(flat_g[order])
    blk_start = jnp.arange(n_blocks) * MOE_BLOCK
    blk_expert = jnp.minimum(jnp.sum(pad_end[None, :] <= blk_start[:, None], axis=1), N_EXPERTS - 1)
    x_pad = jnp.concatenate([xt, jnp.zeros((1, shp[-1]), xt.dtype)], axis=0)

    def expert_block(args):
        tok, e = args
        h = x_pad[tok] @ w_up[e] + b_up[e]
        gate = jnp.minimum(h[:, :D_FF], SWIGLU_LIMIT)
        up = jnp.clip(h[:, D_FF:], -SWIGLU_LIMIT, SWIGLU_LIMIT)
        act = gate * jax.nn.sigmoid(SWIGLU_ALPHA * gate) * (up + 1)
        return act @ w_down[e] + b_down[e]

    yb = lax.map(expert_block, (slot_tok.reshape(n_blocks, MOE_BLOCK), blk_expert))
    yw = yb.reshape(-1, shp[-1]).astype(jnp.float32) * slot_gate[:, None]
    out = jnp.zeros((T + 1, shp[-1]), jnp.float32).at[slot_tok].add(yw)[:T]
    return out.astype(x.dtype).reshape(shp)


def trunk_layer(x, c, pos, pool_prev, ssm_h0_re, ssm_h0_im, kv_past, p, lam_init):
    Bn, L = x.shape[0], x.shape[1]
    mod = jax.nn.silu(c) @ p['w_ada'] + p['b_ada']
    shift1, scale1, gate1, shift2, scale2, gate2 = jnp.split(mod, 6, axis=-1)
    h = ada_rms_norm(x, p['norm1_g'], shift1, scale1)
    u = h @ p['w_in']
    u_pool, q, k, v, u_ssm = jnp.split(u, IN_SPLITS, axis=-1)

    y_pool, pool_new = pool_mixer(u_pool, pool_prev, pos, p['w_pool_mix'], p['pool_scale'])
    y_pool = y_pool @ p['w_pool_out']

    q = rotary(rms_norm(q.reshape(Bn, L, ATT_HEADS, 2, QK_DIM), p['q_norm_g']), pos)
    k = rotary(rms_norm(k.reshape(Bn, L, ATT_HEADS, 2, QK_DIM), p['k_norm_g']), pos)
    v = v.reshape(Bn, L, ATT_HEADS, V_DIM)
    lp = p['diff_lambda'].astype(jnp.float32)
    lam = jnp.exp(jnp.sum(lp[0] * lp[1])) - jnp.exp(jnp.sum(lp[2] * lp[3])) + lam_init
    if kv_past is None:
        o = diff_attn_prompt(q, k, v, lam)
    else:
        k_past, v_past = kv_past
        k_all = jnp.concatenate([k_past.reshape(Bn, -1, ATT_HEADS, 2, QK_DIM).astype(k.dtype), k], axis=1)
        v_all = jnp.concatenate([v_past.astype(v.dtype), v], axis=1)
        o = diff_attend(q, k_all, v_all, None, lam)
    o = rms_norm(o, p['subln_g']) * (1.0 - lam_init)
    y_attn = o.reshape(Bn, L, ATT_WIDTH).astype(x.dtype) @ p['w_attn_out']

    y_ssm, ssm_re, ssm_im = s5_mixer(u_ssm, ssm_h0_re, ssm_h0_im, p['ssm_lambda_re'], p['ssm_lambda_im'],
                                     p['ssm_log_dt'], p['ssm_b_re'], p['ssm_b_im'], p['ssm_c_re'],
                                     p['ssm_c_im'], p['ssm_d'], p['w_glu'], p['b_glu'])
    y_ssm = y_ssm @ p['w_ssm_out']

    g = jax.nn.sigmoid((h @ p['w_gate'] + p['b_gate']).astype(jnp.float32)).astype(x.dtype)
    g = g.reshape(Bn, L, N_BRANCH, D_MODEL)
    merged = g[:, :, 0] * y_pool + g[:, :, 1] * y_attn + g[:, :, 2] * y_ssm
    x = x + gate1[:, None, :] * (merged @ p['w_o'])

    h2 = ada_rms_norm(x, p['norm2_g'], shift2, scale2)
    x = x + gate2[:, None, :] * moe_ffn(h2, p['w_router'], p['b_router'], p['w_up'], p['b_up'],
                                       p['w_down'], p['b_down'])
    k_rows = k.reshape(Bn, L, ATT_HEADS, 2 * QK_DIM)
    return x, k_rows, v, pool_new, ssm_re, ssm_im


def setup_inputs(seed: int = 0) -> dict:
    key = jax.random.key(seed)
    ks = iter(jax.random.split(key, 64))
    f32 = jnp.float32

    def nrm(shape, scale):
        return jax.random.normal(next(ks), shape, f32) * scale

    D = D_MODEL
    x_prompt = nrm((BATCH, SEQ, D), 1.0)
    x_sample = nrm((DEC_BATCH, DEC_SEQ, D), 1.0)
    c_prompt = nrm((BATCH, D), 1.0)
    c_sample = nrm((DEC_BATCH, D), 1.0)
    cache_k = nrm((DEPTH, DEC_BATCH, PAST_LEN, ATT_HEADS, 2 * QK_DIM), 1.0)
    cache_v = nrm((DEPTH, DEC_BATCH, PAST_LEN, ATT_HEADS, V_DIM), 1.0)
    cache_pool = nrm((DEPTH, DEC_BATCH, POOL_HIST, POOL_WIDTH), 1.0)
    state_ssm_re = nrm((DEPTH, DEC_BATCH, SSM_GROUPS, SSM_STATE), 0.1)
    state_ssm_im = nrm((DEPTH, DEC_BATCH, SSM_GROUPS, SSM_STATE), 0.1)

    w_ada = nrm((DEPTH, D, 6 * D), 0.5 * D ** -0.5)
    b_ada = nrm((DEPTH, 6 * D), 0.01)
    norm1_g = 1.0 + nrm((DEPTH, D), 0.1)
    w_in = nrm((DEPTH, D, IN_WIDTH), D ** -0.5)
    w_gate = nrm((DEPTH, D, N_BRANCH * D), D ** -0.5)
    b_gate = nrm((DEPTH, N_BRANCH * D), 0.01)
    w_pool_mix = nrm((DEPTH, POOL_GROUPS, POOL_GW, POOL_GW), POOL_GW ** -0.5)
    pool_scale = 1.0 + nrm((DEPTH, POOL_WIDTH), 0.1)
    w_pool_out = nrm((DEPTH, POOL_WIDTH, D), POOL_WIDTH ** -0.5)
    q_norm_g = 1.0 + nrm((DEPTH, QK_DIM), 0.1)
    k_norm_g = 1.0 + nrm((DEPTH, QK_DIM), 0.1)
    diff_lambda = nrm((DEPTH, 4, QK_DIM), 0.1)
    subln_g = 1.0 + nrm((DEPTH, V_DIM), 0.1)
    w_attn_out = nrm((DEPTH, ATT_WIDTH, D), ATT_WIDTH ** -0.5)
    n_idx = jnp.arange(SSM_STATE, dtype=f32)
    ssm_lambda_re = -0.5 + nrm((DEPTH, SSM_GROUPS, SSM_STATE), 0.01)
    ssm_lambda_im = math.pi * n_idx + nrm((DEPTH, SSM_GROUPS, SSM_STATE), 0.01)
    ssm_log_dt = jax.random.uniform(next(ks), (DEPTH, SSM_GROUPS), f32, math.log(1e-3), math.log(1e-1))
    ssm_b_re = nrm((DEPTH, SSM_GROUPS, SSM_STATE, SSM_GROUP), (2 * SSM_GROUP) ** -0.5)
    ssm_b_im = nrm((DEPTH, SSM_GROUPS, SSM_STATE, SSM_GROUP), (2 * SSM_GROUP) ** -0.5)
    ssm_c_re = nrm((DEPTH, SSM_GROUPS, SSM_GROUP, SSM_STATE), SSM_STATE ** -0.5)
    ssm_c_im = nrm((DEPTH, SSM_GROUPS, SSM_GROUP, SSM_STATE), SSM_STATE ** -0.5)
    ssm_d = nrm((DEPTH, SSM_WIDTH), 1.0)
    w_glu = nrm((DEPTH, SSM_WIDTH, SSM_WIDTH), SSM_WIDTH ** -0.5)
    b_glu = nrm((DEPTH, SSM_WIDTH), 0.01)
    w_ssm_out = nrm((DEPTH, SSM_WIDTH, D), SSM_WIDTH ** -0.5)
    w_o = nrm((DEPTH, D, D), D ** -0.5)
    norm2_g = 1.0 + nrm((DEPTH, D), 0.1)
    w_router = nrm((DEPTH, D, N_EXPERTS), D ** -0.5)
    b_router = nrm((DEPTH, N_EXPERTS), 0.01)
    w_up = nrm((DEPTH, N_EXPERTS, D, 2 * D_FF), D ** -0.5)
    b_up = nrm((DEPTH, N_EXPERTS, 2 * D_FF), 0.01)
    w_down = nrm((DEPTH, N_EXPERTS, D_FF, D), D_FF ** -0.5)
    b_down = nrm((DEPTH, N_EXPERTS, D), 0.01)
    return {'x_prompt': x_prompt, 'x_sample': x_sample, 'c_prompt': c_prompt, 'c_sample': c_sample,
            'cache_k': cache_k, 'cache_v': cache_v, 'cache_pool': cache_pool,
            'state_ssm_re': state_ssm_re, 'state_ssm_im': state_ssm_im,
            'w_ada': w_ada, 'b_ada': b_ada, 'norm1_g': norm1_g, 'w_in': w_in,
            'w_gate': w_gate, 'b_gate': b_gate,
            'w_pool_mix': w_pool_mix, 'pool_scale': pool_scale, 'w_pool_out': w_pool_out,
            'q_norm_g': q_norm_g, 'k_norm_g': k_norm_g, 'diff_lambda': diff_lambda, 'subln_g': subln_g,
            'w_attn_out': w_attn_out,
            'ssm_lambda_re': ssm_lambda_re, 'ssm_lambda_im': ssm_lambda_im, 'ssm_log_dt': ssm_log_dt,
            'ssm_b_re': ssm_b_re, 'ssm_b_im': ssm_b_im, 'ssm_c_re': ssm_c_re, 'ssm_c_im': ssm_c_im,
            'ssm_d': ssm_d, 'w_glu': w_glu, 'b_glu': b_glu, 'w_ssm_out': w_ssm_out,
            'w_o': w_o, 'norm2_g': norm2_g, 'w_router': w_router, 'b_router': b_router,
            'w_up': w_up, 'b_up': b_up, 'w_down': w_down, 'b_down': b_down}


def reference(x_prompt, x_sample, c_prompt, c_sample, cache_k, cache_v, cache_pool, state_ssm_re, state_ssm_im,
              w_ada, b_ada, norm1_g, w_in, w_gate, b_gate, w_pool_mix, pool_scale, w_pool_out,
              q_norm_g, k_norm_g, diff_lambda, subln_g, w_attn_out,
              ssm_lambda_re, ssm_lambda_im, ssm_log_dt, ssm_b_re, ssm_b_im, ssm_c_re, ssm_c_im,
              ssm_d, w_glu, b_glu, w_ssm_out, w_o, norm2_g, w_router, b_router, w_up, b_up, w_down, b_down):
    n_b, seq = x_prompt.shape[0], x_prompt.shape[1]
    dec_seq = x_sample.shape[1]
    past = cache_k.shape[2]
    pos_p = jnp.arange(seq, dtype=jnp.int32)
    pos_s = past + jnp.arange(dec_seq, dtype=jnp.int32)
    pool_zero = jnp.zeros((n_b, POOL_HIST, POOL_WIDTH), x_prompt.dtype)
    ssm_zero = jnp.zeros((n_b, SSM_GROUPS, SSM_STATE), jnp.float32)
    xp, xs = x_prompt, x_sample
    kp_l, vp_l, pp_l, srp_l, sip_l = [], [], [], [], []
    ks_l, vs_l, ps_l, srs_l, sis_l = [], [], [], [], []
    for l in range(DEPTH):
        p = {'w_ada': w_ada[l], 'b_ada': b_ada[l], 'norm1_g': norm1_g[l], 'w_in': w_in[l],
             'w_gate': w_gate[l], 'b_gate': b_gate[l],
             'w_pool_mix': w_pool_mix[l], 'pool_scale': pool_scale[l], 'w_pool_out': w_pool_out[l],
             'q_norm_g': q_norm_g[l], 'k_norm_g': k_norm_g[l], 'diff_lambda': diff_lambda[l],
             'subln_g': subln_g[l], 'w_attn_out': w_attn_out[l],
             'ssm_lambda_re': ssm_lambda_re[l], 'ssm_lambda_im': ssm_lambda_im[l], 'ssm_log_dt': ssm_log_dt[l],
             'ssm_b_re': ssm_b_re[l], 'ssm_b_im': ssm_b_im[l], 'ssm_c_re': ssm_c_re[l], 'ssm_c_im': ssm_c_im[l],
             'ssm_d': ssm_d[l], 'w_glu': w_glu[l], 'b_glu': b_glu[l], 'w_ssm_out': w_ssm_out[l],
             'w_o': w_o[l], 'norm2_g': norm2_g[l], 'w_router': w_router[l], 'b_router': b_router[l],
             'w_up': w_up[l], 'b_up': b_up[l], 'w_down': w_down[l], 'b_down': b_down[l]}
        lam_init = 0.8 - 0.6 * math.exp(-0.3 * l)
        xp, kp, vp, pp, srp, sip = trunk_layer(xp, c_prompt, pos_p, pool_zero, ssm_zero, ssm_zero,
                                               None, p, lam_init)
        xs, ks, vs, ps, srs, sis = trunk_layer(xs, c_sample, pos_s, cache_pool[l], state_ssm_re[l],
                                               state_ssm_im[l], (cache_k[l], cache_v[l]), p, lam_init)
        kp_l.append(kp); vp_l.append(vp); pp_l.append(pp); srp_l.append(srp); sip_l.append(sip)
        ks_l.append(ks); vs_l.append(vs); ps_l.append(ps); srs_l.append(srs); sis_l.append(sis)
    return (xp, xs,
            jnp.stack(kp_l), jnp.stack(vp_l), jnp.stack(pp_l), jnp.stack(srp_l), jnp.stack(sip_l),
            jnp.stack(ks_l), jnp.stack(vs_l), jnp.stack(ps_l), jnp.stack(srs_l), jnp.stack(sis_l))
```

```python
import functools
import math

import jax
import jax.numpy as jnp
from jax import lax
from jax.experimental import pallas as pl
from jax.experimental.pallas import tpu as pltpu

F32 = jnp.float32
BF16 = jnp.bfloat16
HIGHEST = lax.Precision.HIGHEST

RMS_EPS = 1e-6
NEG_INF = -1e30
CHUNK = 64
POOL_WINDOWS = (2, 4, 8, 16)
POOL_HIST = 15
HIST_ROWS = 16
ATT_HEADS = 8
QK_DIM = 64
V_DIM = 128
ROT_DIM = 16
ROPE_THETA = 500000.0
SSM_GROUP = 16
SSM_STATE = 64
SSM_CHUNK = 16
TOP_K = 4
SWIGLU_LIMIT = 7.0
SWIGLU_ALPHA = 1.702
LANES = 128
ROUTER_PAD = 128
VMEM_LIMIT = 52 * 1024 * 1024


def _params(sem, vmem=VMEM_LIMIT):
    return pltpu.CompilerParams(dimension_semantics=sem, vmem_limit_bytes=vmem)


def _mod_operand(arr, tm, rows_per_mod):
    nb, d = arr.shape
    if rows_per_mod % tm == 0:
        return arr.reshape(nb, 1, d), pl.BlockSpec((None, 1, d), lambda i, *_: ((i * tm) // rows_per_mod, 0, 0))
    return jnp.repeat(arr, rows_per_mod, axis=0), pl.BlockSpec((tm, d), lambda i, *_: (i, 0))


def _ada_kernel(c_ref, w_ref, b_ref, o_ref):
    c = c_ref[...]
    s = c * jax.nn.sigmoid(c)
    o_ref[...] = jnp.dot(s.astype(BF16), w_ref[...].astype(BF16), preferred_element_type=F32) + b_ref[...]


def ada_mod(c_pad, w_ada, b_ada):
    depth, d, n = w_ada.shape
    rows = c_pad.shape[0]
    tn = 1024
    return pl.pallas_call(
        _ada_kernel,
        grid=(depth, n // tn),
        in_specs=[pl.BlockSpec((rows, d), lambda l, j: (0, 0)),
                  pl.BlockSpec((None, d, tn), lambda l, j: (l, 0, j)),
                  pl.BlockSpec((None, 1, tn), lambda l, j: (l, 0, j))],
        out_specs=pl.BlockSpec((None, rows, tn), lambda l, j: (l, 0, j)),
        out_shape=jax.ShapeDtypeStruct((depth, rows, n), F32),
        compiler_params=_params(("parallel", "parallel")),
        name="ada_mod",
    )(c_pad, w_ada, b_ada.reshape(depth, 1, n))


def _norm_mm_kernel(x_ref, g_ref, sh_ref, sc_ref, w_ref, b_ref, o_ref, h_scr, *, act):
    @pl.when(pl.program_id(1) == 0)
    def _():
        x = x_ref[...]
        ms = jnp.mean(x * x, axis=-1, keepdims=True)
        y = x * lax.rsqrt(ms + RMS_EPS) * g_ref[...]
        h_scr[...] = (y * (1.0 + sc_ref[...]) + sh_ref[...]).astype(BF16)

    r = jnp.dot(h_scr[...], w_ref[...], preferred_element_type=F32) + b_ref[...]
    if act == "sigmoid":
        r = jax.nn.sigmoid(r)
    o_ref[...] = r.astype(o_ref.dtype)


def norm_matmul(x, gain, shift, scale, w, bias, *, rows_per_mod, act, out_dtype, tm):
    t, d = x.shape
    n = w.shape[1]
    tn = 1024
    sh, sh_spec = _mod_operand(shift, tm, rows_per_mod)
    sc, sc_spec = _mod_operand(scale, tm, rows_per_mod)
    return pl.pallas_call(
        functools.partial(_norm_mm_kernel, act=act),
        grid=(t // tm, n // tn),
        in_specs=[pl.BlockSpec((tm, d), lambda i, j: (i, 0)),
                  pl.BlockSpec((1, d), lambda i, j: (0, 0)),
                  sh_spec, sc_spec,
                  pl.BlockSpec((d, tn), lambda i, j: (0, j)),
                  pl.BlockSpec((1, tn), lambda i, j: (0, j))],
        out_specs=pl.BlockSpec((tm, tn), lambda i, j: (i, j)),
        out_shape=jax.ShapeDtypeStruct((t, n), out_dtype),
        scratch_shapes=[pltpu.VMEM((tm, d), BF16)],
        compiler_params=_params(("parallel", "arbitrary")),
        name="norm_matmul_" + act,
    )(x, gain.reshape(1, d), sh, sc, w, bias.reshape(1, n))


def _rope_tables(pos):
    half = ROT_DIM // 2
    inv_freq = jnp.exp(-math.log(ROPE_THETA) * jnp.arange(half, dtype=F32) * (2.0 / ROT_DIM))
    ang = pos.astype(F32)[:, None] * inv_freq[None, :]
    cos, sin = jnp.cos(ang), jnp.sin(ang)
    lane = jnp.arange(LANES) % QK_DIM
    idx = lane % half
    c_t = jnp.where(lane[None, :] < ROT_DIM, cos[:, idx], 1.0)
    s1_t = jnp.where((lane[None, :] >= half) & (lane[None, :] < ROT_DIM), sin[:, idx], 0.0)
    s2_t = jnp.where(lane[None, :] < half, -sin[:, idx], 0.0)
    return c_t.astype(F32), s1_t.astype(F32), s2_t.astype(F32)


def _qk_prep_kernel(q_ref, k_ref, v_ref, qg_ref, kg_ref, c_ref, s1_ref, s2_ref, gm_ref,
                    q2_ref, kb_ref, vb_ref, krow_ref, vrow_ref):
    c_t, s1_t, s2_t = c_ref[...], s1_ref[...], s2_ref[...]
    gm = gm_ref[...]
    lane = lax.broadcasted_iota(jnp.int32, (1, LANES), 1)
    first_map = lane < QK_DIM

    def norm_rot(x, g):
        ms = jnp.dot(x * x, gm, preferred_element_type=F32, precision=HIGHEST)
        y = x * lax.rsqrt(ms + RMS_EPS) * g
        return y * c_t + pltpu.roll(y, ROT_DIM // 2, 1) * s1_t + pltpu.roll(y, LANES - ROT_DIM // 2, 1) * s2_t

    for h in range(ATT_HEADS):
        sl = slice(h * LANES, (h + 1) * LANES)
        qr = norm_rot(q_ref[:, sl], qg_ref[...]) * (QK_DIM ** -0.5)
        q2_ref[h, 0] = jnp.where(first_map, qr, 0.0).astype(BF16)
        q2_ref[h, 1] = jnp.where(first_map, 0.0, qr).astype(BF16)
        kr = norm_rot(k_ref[:, sl], kg_ref[...])
        krow_ref[:, sl] = kr
        kb_ref[h] = kr.astype(BF16)
        v = v_ref[:, sl]
        vb_ref[h] = v.astype(BF16)
    vrow_ref[...] = v_ref[...]


def qk_prep(u3, q_norm_g, k_norm_g, pos, tm):
    b, l, _ = u3.shape
    aw = ATT_HEADS * V_DIM
    c_t, s1_t, s2_t = _rope_tables(pos)
    qg = jnp.tile(q_norm_g, 2).reshape(1, LANES)
    kg = jnp.tile(k_norm_g, 2).reshape(1, LANES)
    grp = jnp.arange(LANES) // QK_DIM
    gm = jnp.where(grp[:, None] == grp[None, :], 1.0 / QK_DIM, 0.0).astype(F32)
    tab = pl.BlockSpec((tm, LANES), lambda bi, i: (i, 0))
    vec = pl.BlockSpec((1, LANES), lambda bi, i: (0, 0))
    return pl.pallas_call(
        _qk_prep_kernel,
        grid=(b, l // tm),
        in_specs=[pl.BlockSpec((None, tm, aw), lambda bi, i: (bi, i, 0)),
                  pl.BlockSpec((None, tm, aw), lambda bi, i: (bi, i, 1)),
                  pl.BlockSpec((None, tm, aw), lambda bi, i: (bi, i, 2)),
                  vec, vec, tab, tab, tab,
                  pl.BlockSpec((LANES, LANES), lambda bi, i: (0, 0))],
        out_specs=[pl.BlockSpec((None, ATT_HEADS, 2, tm, LANES), lambda bi, i: (bi, 0, 0, i, 0)),
                   pl.BlockSpec((None, ATT_HEADS, tm, LANES), lambda bi, i: (bi, 0, i, 0)),
                   pl.BlockSpec((None, ATT_HEADS, tm, LANES), lambda bi, i: (bi, 0, i, 0)),
                   pl.BlockSpec((None, tm, aw), lambda bi, i: (bi, i, 0)),
                   pl.BlockSpec((None, tm, aw), lambda bi, i: (bi, i, 0))],
        out_shape=[jax.ShapeDtypeStruct((b, ATT_HEADS, 2, l, LANES), BF16),
                   jax.ShapeDtypeStruct((b, ATT_HEADS, l, LANES), BF16),
                   jax.ShapeDtypeStruct((b, ATT_HEADS, l, LANES), BF16),
                   jax.ShapeDtypeStruct((b, l, aw), F32),
                   jax.ShapeDtypeStruct((b, l, aw), F32)],
        compiler_params=_params(("parallel", "parallel")),
        name="qk_prep",
    )(u3, u3, u3, qg, kg, c_t, s1_t, s2_t, gm)


def _diff_lambda(dl_ref, lam_init):
    dl = dl_ref[...]
    a = jnp.sum(dl[0:1] * dl[1:2], axis=1, keepdims=True)
    b = jnp.sum(dl[2:3] * dl[3:4], axis=1, keepdims=True)
    return jnp.exp(a) - jnp.exp(b) + lam_init


def _diff_finish(acc, l, rows, dl_ref, sg_ref, lam_init):
    o = acc / l
    od = o[:rows] - _diff_lambda(dl_ref, lam_init) * o[rows:]
    ms = jnp.mean(od * od, axis=-1, keepdims=True)
    return od * lax.rsqrt(ms + RMS_EPS) * sg_ref[...] * (1.0 - lam_init)


def _softmax_step(s, v, m, l, acc):
    m_new = jnp.maximum(m, jnp.max(s, axis=1, keepdims=True))
    alpha = jnp.exp(m - m_new)
    p = jnp.exp(s - m_new)
    l_new = alpha * l + jnp.sum(p, axis=1, keepdims=True)
    acc_new = alpha * acc + jnp.dot(p.astype(BF16), v, preferred_element_type=F32)
    return m_new, l_new, acc_new


_NT = (((1,), (1,)), ((), ()))


def _flash_kernel(q_ref, k_ref, v_ref, dl_ref, sg_ref, o_ref, *, tq, lam_init):
    qi = pl.program_id(2)
    q = q_ref[...].reshape(2 * tq, LANES)

    def scores(off):
        k = k_ref[pl.ds(off, tq), :]
        v = v_ref[pl.ds(off, tq), :]
        return lax.dot_general(q, k, _NT, preferred_element_type=F32), v

    def body(kb, carry):
        s, v = scores(pl.multiple_of(kb * tq, tq))
        return _softmax_step(s, v, *carry)

    init = (jnp.full((2 * tq, 1), NEG_INF, F32), jnp.zeros((2 * tq, 1), F32), jnp.zeros((2 * tq, V_DIM), F32))
    carry = lax.fori_loop(0, qi, body, init)

    s, v = scores(pl.multiple_of(qi * tq, tq))
    row = lax.broadcasted_iota(jnp.int32, (2 * tq, tq), 0) & (tq - 1)
    col = lax.broadcasted_iota(jnp.int32, (2 * tq, tq), 1)
    s = jnp.where((col // CHUNK) <= (row // CHUNK), s, NEG_INF)
    _, l, acc = _softmax_step(s, v, *carry)
    o_ref[...] = _diff_finish(acc, l, tq, dl_ref, sg_ref, lam_init).astype(o_ref.dtype)


def flash_diff_attn(q2, kb, vb, diff_lambda, subln_g, lam_init, tq):
    b, h, _, l, _ = q2.shape
    return pl.pallas_call(
        functools.partial(_flash_kernel, tq=tq, lam_init=lam_init),
        grid=(b, h, l // tq),
        in_specs=[pl.BlockSpec((None, None, 2, tq, LANES), lambda bi, hi, qi: (bi, hi, 0, qi, 0)),
                  pl.BlockSpec((None, None, l, LANES), lambda bi, hi, qi: (bi, hi, 0, 0)),
                  pl.BlockSpec((None, None, l, LANES), lambda bi, hi, qi: (bi, hi, 0, 0)),
                  pl.BlockSpec((4, QK_DIM), lambda bi, hi, qi: (0, 0)),
                  pl.BlockSpec((1, V_DIM), lambda bi, hi, qi: (0, 0))],
        out_specs=pl.BlockSpec((None, tq, V_DIM), lambda bi, hi, qi: (bi, qi, hi)),
        out_shape=jax.ShapeDtypeStruct((b, l, h * V_DIM), BF16),
        compiler_params=_params(("parallel", "parallel", "arbitrary")),
        name="flash_diff_attn",
    )(q2, kb, vb, diff_lambda, subln_g.reshape(1, V_DIM))


def _cached_attn_kernel(q_ref, ck_ref, cv_ref, k_ref, v_ref, dl_ref, sg_ref, o_ref, *, rows, lam_init):
    q = q_ref[...].reshape(2 * rows, LANES)
    s_c = lax.dot_general(q, ck_ref[...].astype(BF16), _NT, preferred_element_type=F32)
    s_n = lax.dot_general(q, k_ref[...], _NT, preferred_element_type=F32)
    m = jnp.maximum(jnp.max(s_c, axis=1, keepdims=True), jnp.max(s_n, axis=1, keepdims=True))
    p_c = jnp.exp(s_c - m)
    p_n = jnp.exp(s_n - m)
    l = jnp.sum(p_c, axis=1, keepdims=True) + jnp.sum(p_n, axis=1, keepdims=True)
    acc = (jnp.dot(p_c.astype(BF16), cv_ref[...].astype(BF16), preferred_element_type=F32)
           + jnp.dot(p_n.astype(BF16), v_ref[...], preferred_element_type=F32))
    o_ref[...] = _diff_finish(acc, l, rows, dl_ref, sg_ref, lam_init).astype(o_ref.dtype)


def cached_diff_attn(q2, kb, vb, cache_k, cache_v, layer, diff_lambda, subln_g, lam_init):
    b, h, _, rows, _ = q2.shape
    past = cache_k.shape[2]
    cache_spec = pl.BlockSpec((None, None, past, LANES), lambda bi, hi: (layer, bi, 0, hi))
    new_spec = pl.BlockSpec((None, None, rows, LANES), lambda bi, hi: (bi, hi, 0, 0))
    return pl.pallas_call(
        functools.partial(_cached_attn_kernel, rows=rows, lam_init=lam_init),
        grid=(b, h),
        in_specs=[pl.BlockSpec((None, None, 2, rows, LANES), lambda bi, hi: (bi, hi, 0, 0, 0)),
                  cache_spec, cache_spec, new_spec, new_spec,
                  pl.BlockSpec((4, QK_DIM), lambda bi, hi: (0, 0)),
                  pl.BlockSpec((1, V_DIM), lambda bi, hi: (0, 0))],
        out_specs=pl.BlockSpec((None, rows, V_DIM), lambda bi, hi: (bi, 0, hi)),
        out_shape=jax.ShapeDtypeStruct((b, rows, h * V_DIM), BF16),
        compiler_params=_params(("parallel", "parallel")),
        name="cached_diff_attn",
    )(q2, cache_k, cache_v, kb, vb, diff_lambda, subln_g.reshape(1, V_DIM))


def _pool_kernel(u_ref, hist_ref, prev_ref, w_ref, sc_ref, o_ref, *, tm, pos0):
    i = pl.program_id(1)
    hist = jnp.where(i == 0, prev_ref[...], hist_ref[...])
    pos = pos0 + i * tm + lax.broadcasted_iota(jnp.int32, (tm, 1), 0)
    gw = LANES
    for g, w in enumerate(POOL_WINDOWS):
        sl = slice(g * gw, (g + 1) * gw)
        ext = jnp.concatenate([hist[:, sl], u_ref[:, sl]], axis=0)
        s = ext
        step = 1
        while step < w:
            s = s + pltpu.roll(s, step, 0)
            step *= 2
        cnt = jnp.minimum(pos + 1, w).astype(F32)
        d = s[HIST_ROWS:] / cnt - ext[HIST_ROWS:]
        y = jnp.dot(d.astype(BF16), w_ref[g], preferred_element_type=F32) * sc_ref[:, sl]
        o_ref[:, sl] = y.astype(o_ref.dtype)


def pool_mix(u3, col_block, prev16, w_mix, pool_scale, pos0, tm):
    b, l, _ = u3.shape
    pw = w_mix.shape[0] * w_mix.shape[1]
    hb = tm // HIST_ROWS
    return pl.pallas_call(
        functools.partial(_pool_kernel, tm=tm, pos0=pos0),
        grid=(b, l // tm),
        in_specs=[pl.BlockSpec((None, tm, pw), lambda bi, i: (bi, i, col_block)),
                  pl.BlockSpec((None, HIST_ROWS, pw), lambda bi, i: (bi, jnp.maximum(i * hb - 1, 0), col_block)),
                  pl.BlockSpec((None, HIST_ROWS, pw), lambda bi, i: (bi, 0, 0)),
                  pl.BlockSpec(w_mix.shape, lambda bi, i: (0, 0, 0)),
                  pl.BlockSpec((1, pw), lambda bi, i: (0, 0))],
        out_specs=pl.BlockSpec((None, tm, pw), lambda bi, i: (bi, i, 0)),
        out_shape=jax.ShapeDtypeStruct((b, l, pw), BF16),
        compiler_params=_params(("parallel", "parallel")),
        name="pool_mix",
    )(u3, u3, prev16, w_mix.astype(BF16), pool_scale.reshape(1, pw))


def _ssm_matrices(lam_re, lam_im, log_dt, b_re, b_im, c_re, c_im, d_skip):
    g, n = lam_re.shape
    cw = SSM_GROUP
    lc = SSM_CHUNK
    dt = jnp.exp(log_dt.astype(F32))[:, None]
    d_idx = jnp.arange(lc + 1, dtype=F32)[None, :, None]
    mag = jnp.exp(lam_re[:, None, :] * dt[:, :, None] * d_idx)
    ang = lam_im[:, None, :] * dt[:, :, None] * d_idx
    pw_re, pw_im = mag * jnp.cos(ang), mag * jnp.sin(ang)
    num_re, num_im = pw_re[:, 1] - 1.0, pw_im[:, 1]
    den = lam_re * lam_re + lam_im * lam_im
    f_re = (num_re * lam_re + num_im * lam_im) / den
    f_im = (num_im * lam_re - num_re * lam_im) / den
    bb_re = f_re[..., None] * b_re - f_im[..., None] * b_im
    bb_im = f_re[..., None] * b_im + f_im[..., None] * b_re
    cp_re = c_re[:, None] * pw_re[:, :, None, :] - c_im[:, None] * pw_im[:, :, None, :]
    cp_im = c_re[:, None] * pw_im[:, :, None, :] + c_im[:, None] * pw_re[:, :, None, :]
    w = (jnp.einsum("gdcn,gne->gdce", cp_re[:, :lc], bb_re, precision=HIGHEST)
         - jnp.einsum("gdcn,gne->gdce", cp_im[:, :lc], bb_im, precision=HIGHEST))
    s_idx = jnp.arange(lc)[:, None]
    t_idx = jnp.arange(lc)[None, :]
    lag = jnp.clip(t_idx - s_idx, 0, lc - 1)
    toe = jnp.where((t_idx >= s_idx)[None, :, :, None, None], w[:, lag], 0.0)
    m = toe.transpose(0, 1, 4, 2, 3).reshape(g, lc * cw, lc * cw)
    dvec = jnp.tile(d_skip.reshape(g, 1, cw), (1, lc, 1)).reshape(g, lc * cw)
    m = m + jnp.eye(lc * cw, dtype=F32)[None] * dvec[:, None, :]
    pr, pi = pw_re[:, lc - 1::-1][:, :lc], pw_im[:, lc - 1::-1][:, :lc]
    p_re = pr[:, :, None, :] * bb_re.transpose(0, 2, 1)[:, None] - pi[:, :, None, :] * bb_im.transpose(0, 2, 1)[:, None]
    p_im = pr[:, :, None, :] * bb_im.transpose(0, 2, 1)[:, None] + pi[:, :, None, :] * bb_re.transpose(0, 2, 1)[:, None]
    p = jnp.concatenate([p_re, p_im], axis=-1).reshape(g, lc * cw, 2 * n)
    q = jnp.concatenate([cp_re[:, 1:].transpose(0, 3, 1, 2), -cp_im[:, 1:].transpose(0, 3, 1, 2)], axis=1)
    q = q.reshape(g, 2 * n, lc * cw)
    a_re, a_im = pw_re[:, lc], pw_im[:, lc]
    a1 = jnp.concatenate([a_re, a_re], axis=-1).reshape(g, 1, 2 * n)
    a2 = jnp.concatenate([-a_im, a_im], axis=-1).reshape(g, 1, 2 * n)
    return m, p, q, a1, a2


def _ssm_s_kernel(u_ref, p_ref, s_ref):
    s_ref[...] = jnp.dot(u_ref[...], p_ref[...], preferred_element_type=F32, precision=HIGHEST)


def _ssm_scan_kernel(s_ref, a1_ref, a2_ref, h0_ref, h_ref, hf_ref, *, nc):
    a1 = a1_ref[...]
    a2 = a2_ref[...]

    def body(c, h):
        h_ref[:, pl.ds(c, 1), :] = h
        s = s_ref[:, pl.ds(c, 1), :]
        return a1 * h + a2 * pltpu.roll(h, SSM_STATE, 2) + s

    hf_ref[...] = lax.fori_loop(0, nc, body, h0_ref[...])


def _ssm_y_kernel(u_ref, h_ref, m_ref, q_ref, y_ref):
    y_ref[...] = (jnp.dot(u_ref[...], m_ref[...], preferred_element_type=F32, precision=HIGHEST)
                  + jnp.dot(h_ref[...], q_ref[...], preferred_element_type=F32, precision=HIGHEST))


def s5_mix(u_ssm, h0_re, h0_im, mats):
    m, p, q, a1, a2 = mats
    b, l, w = u_ssm.shape
    g = w // SSM_GROUP
    n2 = 2 * SSM_STATE
    nc = l // SSM_CHUNK
    rows = b * nc
    kw = SSM_CHUNK * SSM_GROUP
    ug = u_ssm.reshape(rows, SSM_CHUNK, g, SSM_GROUP).transpose(2, 0, 1, 3).reshape(g, rows, kw)
    grp = lambda *shape: pl.BlockSpec((None,) + shape, lambda gi: (gi, 0, 0))
    s = pl.pallas_call(
        _ssm_s_kernel, grid=(g,),
        in_specs=[grp(rows, kw), grp(kw, n2)], out_specs=grp(rows, n2),
        out_shape=jax.ShapeDtypeStruct((g, rows, n2), F32),
        compiler_params=_params(("parallel",)), name="ssm_chunk_state",
    )(ug, p)
    h0 = jnp.concatenate([h0_re, h0_im], axis=-1).transpose(1, 0, 2).reshape(g * b, 1, n2)
    hs, hf = pl.pallas_call(
        functools.partial(_ssm_scan_kernel, nc=nc),
        out_shape=[jax.ShapeDtypeStruct((g * b, nc, n2), F32), jax.ShapeDtypeStruct((g * b, 1, n2), F32)],
        compiler_params=pltpu.CompilerParams(vmem_limit_bytes=VMEM_LIMIT), name="ssm_scan",
    )(s.reshape(g * b, nc, n2), jnp.repeat(a1, b, axis=0), jnp.repeat(a2, b, axis=0), h0)
    y = pl.pallas_call(
        _ssm_y_kernel, grid=(g,),
        in_specs=[grp(rows, kw), grp(rows, n2), grp(kw, kw), grp(n2, kw)], out_specs=grp(rows, kw),
        out_shape=jax.ShapeDtypeStruct((g, rows, kw), F32),
        compiler_params=_params(("parallel",)), name="ssm_chunk_out",
    )(ug, hs.reshape(g, rows, n2), m, q)
    y = y.reshape(g, rows, SSM_CHUNK, SSM_GROUP).transpose(1, 2, 0, 3).reshape(b, l, w)
    hf = hf.reshape(g, b, n2).transpose(1, 0, 2)
    return y, hf[..., :SSM_STATE], hf[..., SSM_STATE:]


def _merge_kernel(pm_ref, oa_ref, ys_ref, g0_ref, g1_ref, g2_ref, x_ref, gate1_ref,
                  wglu_ref, bglu_ref, wpo_ref, wao_ref, wso_ref, wo_ref,
                  n2g_ref, sh2_ref, sc2_ref, wr_ref, br_ref,
                  xo_ref, h2_ref, lg_ref):
    gl = jax.nn.gelu(ys_ref[...])
    z = gl * jax.nn.sigmoid(jnp.dot(gl.astype(BF16), wglu_ref[...], preferred_element_type=F32) + bglu_ref[...])
    merged = (g0_ref[...] * jnp.dot(pm_ref[...], wpo_ref[...], preferred_element_type=F32)
              + g1_ref[...] * jnp.dot(oa_ref[...], wao_ref[...], preferred_element_type=F32)
              + g2_ref[...] * jnp.dot(z.astype(BF16), wso_ref[...], preferred_element_type=F32))
    x = x_ref[...] + gate1_ref[...] * jnp.dot(merged.astype(BF16), wo_ref[...], preferred_element_type=F32)
    xo_ref[...] = x
    ms = jnp.mean(x * x, axis=-1, keepdims=True)
    h2 = x * lax.rsqrt(ms + RMS_EPS) * n2g_ref[...] * (1.0 + sc2_ref[...]) + sh2_ref[...]
    h2_ref[...] = h2.astype(BF16)
    lg_ref[...] = jnp.dot(h2, wr_ref[...], preferred_element_type=F32, precision=HIGHEST) + br_ref[...]


def merge_branches(pm, oa, ys, gates, x, gate1, shift2, scale2, wts, rows_per_mod, tm):
    t, d = x.shape
    g1op, g1_spec = _mod_operand(gate1, tm, rows_per_mod)
    sh2, sh2_spec = _mod_operand(shift2, tm, rows_per_mod)
    sc2, sc2_spec = _mod_operand(scale2, tm, rows_per_mod)

    def rows(width, col=0):
        return pl.BlockSpec((tm, width), lambda i: (i, col))

    def whole(arr):
        return pl.BlockSpec(arr.shape, lambda i: (0,) * arr.ndim, pipeline_mode=pl.Buffered(1))

    consts = [wts["w_glu"], wts["b_glu"], wts["w_pool_out"], wts["w_attn_out"], wts["w_ssm_out"], wts["w_o"]]
    tail = [wts["norm2_g"]]
    tail2 = [wts["w_router"], wts["b_router"]]
    return pl.pallas_call(
        _merge_kernel,
        grid=(t // tm,),
        in_specs=[rows(pm.shape[1]), rows(oa.shape[1]), rows(ys.shape[1]),
                  rows(d, 0), rows(d, 1), rows(d, 2), rows(d), g1_spec]
                 + [whole(a) for a in consts] + [whole(a) for a in tail] + [sh2_spec, sc2_spec]
                 + [whole(a) for a in tail2],
        out_specs=[rows(d), rows(d), rows(ROUTER_PAD)],
        out_shape=[jax.ShapeDtypeStruct((t, d), F32), jax.ShapeDtypeStruct((t, d), BF16),
                   jax.ShapeDtypeStruct((t, ROUTER_PAD), F32)],
        compiler_params=_params(("parallel",)),
        name="merge_branches",
    )(pm, oa, ys, gates, gates, gates, x, g1op, *consts, *tail, sh2, sc2, *tail2)


def _moe_kernel(be_ref, nv_ref, x_ref, wg_ref, wu_ref, bg_ref, bu_ref, wd_ref, bd_ref, o_ref, acc_ref, *, nf):
    i, f = pl.program_id(0), pl.program_id(1)

    @pl.when(i < nv_ref[0])
    def _():
        x = x_ref[...]
        hg = jnp.dot(x, wg_ref[...], preferred_element_type=F32) + bg_ref[...]
        hu = jnp.dot(x, wu_ref[...], preferred_element_type=F32) + bu_ref[...]
        gate = jnp.minimum(hg, SWIGLU_LIMIT)
        up = jnp.clip(hu, -SWIGLU_LIMIT, SWIGLU_LIMIT)
        act = gate * jax.nn.sigmoid(SWIGLU_ALPHA * gate) * (up + 1.0)
        part = jnp.dot(act.astype(BF16), wd_ref[...], preferred_element_type=F32)

        @pl.when(f == 0)
        def _():
            acc_ref[...] = part

        @pl.when(f > 0)
        def _():
            acc_ref[...] += part

        @pl.when(f == nf - 1)
        def _():
            o_ref[...] = (acc_ref[...] + bd_ref[...]).astype(o_ref.dtype)


def moe_experts(x_sorted, blk_expert, n_valid, w_up, b_up, w_down, b_down, tm, tf):
    s, d = x_sorted.shape
    e, _, f2 = w_up.shape
    ff = f2 // 2
    nf = ff // tf

    def blk(i, nv):
        return jnp.minimum(i, nv[0] - 1)

    grid_spec = pltpu.PrefetchScalarGridSpec(
        num_scalar_prefetch=2,
        grid=(s // tm, nf),
        in_specs=[pl.BlockSpec((tm, d), lambda i, f, be, nv: (blk(i, nv), 0)),
                  pl.BlockSpec((None, d, tf), lambda i, f, be, nv: (be[blk(i, nv)], 0, f)),
                  pl.BlockSpec((None, d, tf), lambda i, f, be, nv: (be[blk(i, nv)], 0, nf + f)),
                  pl.BlockSpec((None, 1, tf), lambda i, f, be, nv: (be[blk(i, nv)], 0, f)),
                  pl.BlockSpec((None, 1, tf), lambda i, f, be, nv: (be[blk(i, nv)], 0, nf + f)),
                  pl.BlockSpec((None, tf, d), lambda i, f, be, nv: (be[blk(i, nv)], f, 0)),
                  pl.BlockSpec((None, 1, d), lambda i, f, be, nv: (be[blk(i, nv)], 0, 0))],
        out_specs=pl.BlockSpec((tm, d), lambda i, f, be, nv: (blk(i, nv), 0)),
        scratch_shapes=[pltpu.VMEM((tm, d), F32)],
    )
    return pl.pallas_call(
        functools.partial(_moe_kernel, nf=nf),
        grid_spec=grid_spec,
        out_shape=jax.ShapeDtypeStruct((s, d), BF16),
        compiler_params=_params(("arbitrary", "arbitrary")),
        name="moe_experts",
    )(blk_expert, n_valid, x_sorted, w_up, w_up, b_up.reshape(e, 1, f2), b_up.reshape(e, 1, f2),
      w_down, b_down.reshape(e, 1, d))


def _combine_kernel(x_ref, y_ref, gt_ref, gate2_ref, o_ref, *, d):
    gt = gt_ref[...]
    acc = gt[:, 0:1] * y_ref[:, 0:d].astype(F32)
    for k in range(1, TOP_K):
        acc = acc + gt[:, k:k + 1] * y_ref[:, k * d:(k + 1) * d].astype(F32)
    o_ref[...] = x_ref[...] + gate2_ref[...] * acc


def moe_combine(x, y_gathered, gates, gate2, rows_per_mod, tm):
    t, d = x.shape
    g2op, g2_spec = _mod_operand(gate2, tm, rows_per_mod)
    return pl.pallas_call(
        functools.partial(_combine_kernel, d=d),
        grid=(t // tm,),
        in_specs=[pl.BlockSpec((tm, d), lambda i: (i, 0)),
                  pl.BlockSpec((tm, TOP_K * d), lambda i: (i, 0)),
                  pl.BlockSpec((tm, TOP_K), lambda i: (i, 0)),
                  g2_spec],
        out_specs=pl.BlockSpec((tm, d), lambda i: (i, 0)),
        out_shape=jax.ShapeDtypeStruct((t, d), F32),
        compiler_params=_params(("parallel",)),
        name="moe_combine",
    )(x, y_gathered, gates, g2op)


def _route(logits, n_experts, tm):
    t = logits.shape[0]
    top_v, top_i = lax.top_k(logits, TOP_K)
    gates = jax.nn.softmax(top_v, axis=-1)
    tk = t * TOP_K
    flat_e = top_i.reshape(tk)
    onehot = (flat_e[:, None] == jnp.arange(n_experts, dtype=flat_e.dtype)[None, :]).astype(jnp.int32)
    csum = jnp.cumsum(onehot, axis=0)
    rank = jnp.take_along_axis(csum, flat_e[:, None], axis=1)[:, 0] - 1
    counts = csum[-1]
    padded = (counts + tm - 1) // tm * tm
    pad_end = jnp.cumsum(padded)
    pad_start = pad_end - padded
    dest = pad_start[flat_e] + rank
    n_blocks = -(-tk // tm) + n_experts
    flat_t = jnp.repeat(jnp.arange(t, dtype=jnp.int32), TOP_K)
    slot_tok = jnp.zeros((n_blocks * tm,), jnp.int32).at[dest].set(flat_t)
    blk_start = jnp.arange(n_blocks, dtype=jnp.int32) * tm
    blk_expert = jnp.minimum(jnp.sum(pad_end[None, :] <= blk_start[:, None], axis=1), n_experts - 1).astype(jnp.int32)
    n_valid = (pad_end[-1] // tm).astype(jnp.int32).reshape(1)
    return slot_tok, dest.reshape(t, TOP_K), gates, blk_expert, n_valid


def _pick_tile(n, pref):
    return pref if n % pref == 0 else n


def _mixers(x, mods, pos0, pool_prev16, h0_re, h0_im, kv_cache, layer, wts, ssm_mats, lam_init):
    b, l, d = x.shape
    t = b * l
    shift1, scale1, gate1, shift2, scale2, _ = mods
    xt = x.reshape(t, d)
    tm = _pick_tile(t, 512)
    u = norm_matmul(xt, wts["norm1_g"], shift1, scale1, wts["w_in"], wts["b_in"],
                    rows_per_mod=l, act="none", out_dtype=F32, tm=tm)
    gates = norm_matmul(xt, wts["norm1_g"], shift1, scale1, wts["w_gate"], wts["b_gate"],
                        rows_per_mod=l, act="sigmoid", out_dtype=BF16, tm=tm)
    u3 = u.reshape(b, l, u.shape[1])
    pos = pos0 + jnp.arange(l, dtype=jnp.int32)
    tl = _pick_tile(l, 512)
    q2, kb, vb, k_rows, v_rows = qk_prep(u3, wts["q_norm_g"], wts["k_norm_g"], pos, tl)
    if kv_cache is None:
        oa = flash_diff_attn(q2, kb, vb, wts["diff_lambda"], wts["subln_g"], lam_init, tl)
    else:
        oa = cached_diff_attn(q2, kb, vb, kv_cache[0], kv_cache[1], layer, wts["diff_lambda"], wts["subln_g"],
                              lam_init)
    aw = ATT_HEADS * V_DIM
    pw = wts["pool_scale"].shape[0]
    pm = pool_mix(u3, (3 * aw) // pw, pool_prev16, wts["w_pool_mix"], wts["pool_scale"], pos0, tl)
    pool_new = u3[:, l - POOL_HIST:, 3 * aw:3 * aw + pw]
    ys, ssm_re, ssm_im = s5_mix(u3[:, :, 3 * aw + pw:], h0_re, h0_im, ssm_mats)
    x_new, h2, logits = merge_branches(pm.reshape(t, pw), oa.reshape(t, aw), ys.reshape(t, -1), gates, xt,
                                       gate1, shift2, scale2, wts, l, _pick_tile(t, 256))
    outs = (k_rows.reshape(b, l, ATT_HEADS, 2 * QK_DIM), v_rows.reshape(b, l, ATT_HEADS, V_DIM),
            pool_new, ssm_re, ssm_im)
    return x_new, h2, logits, outs


def kernel(x_prompt, x_sample, c_prompt, c_sample, cache_k, cache_v, cache_pool, state_ssm_re, state_ssm_im, w_ada, b_ada, norm1_g, w_in, w_gate, b_gate, w_pool_mix, pool_scale, w_pool_out, q_norm_g, k_norm_g, diff_lambda, subln_g, w_attn_out, ssm_lambda_re, ssm_lambda_im, ssm_log_dt, ssm_b_re, ssm_b_im, ssm_c_re, ssm_c_im, ssm_d, w_glu, b_glu, w_ssm_out, w_o, norm2_g, w_router, b_router, w_up, b_up, w_down, b_down):
    nb, seq, d = x_prompt.shape
    db, dseq, _ = x_sample.shape
    depth = w_ada.shape[0]
    past = cache_k.shape[2]
    n_exp = w_router.shape[-1]
    aw = ATT_HEADS * V_DIM
    pw = pool_scale.shape[-1]
    tp, ts = nb * seq, db * dseq

    n_c = nb + db
    c_rows = -(-n_c // 8) * 8
    c_pad = jnp.zeros((c_rows, d), F32).at[:n_c].set(jnp.concatenate([c_prompt, c_sample], axis=0))
    mod_all = ada_mod(c_pad, w_ada, b_ada)

    cache_k4 = cache_k.reshape(depth, db, past, aw)
    cache_v4 = cache_v.reshape(depth, db, past, aw)
    pool_zero = jnp.zeros((nb, HIST_ROWS, pw), F32)
    ssm_zero = jnp.zeros((nb,) + state_ssm_re.shape[2:], F32)
    pad_hist = HIST_ROWS - POOL_HIST
    moe_tm = 512

    xp, xs = x_prompt, x_sample
    outs_p, outs_s = [], []
    for l in range(depth):
        lam_init = 0.8 - 0.6 * math.exp(-0.3 * l)
        wi = w_in[l]
        w_in_r = jnp.concatenate([wi[:, pw:pw + 3 * aw], wi[:, :pw], wi[:, pw + 3 * aw:]], axis=1).astype(BF16)
        wr = jnp.zeros((d, ROUTER_PAD), F32).at[:, :n_exp].set(w_router[l])
        br = jnp.zeros((1, ROUTER_PAD), F32).at[0, :n_exp].set(b_router[l])
        wts = {
            "norm1_g": norm1_g[l], "w_in": w_in_r, "b_in": jnp.zeros((w_in_r.shape[1],), F32),
            "w_gate": w_gate[l].astype(BF16), "b_gate": b_gate[l],
            "w_pool_mix": w_pool_mix[l], "pool_scale": pool_scale[l],
            "q_norm_g": q_norm_g[l], "k_norm_g": k_norm_g[l], "diff_lambda": diff_lambda[l], "subln_g": subln_g[l],
            "w_glu": w_glu[l].astype(BF16), "b_glu": b_glu[l].reshape(1, -1),
            "w_pool_out": w_pool_out[l].astype(BF16), "w_attn_out": w_attn_out[l].astype(BF16),
            "w_ssm_out": w_ssm_out[l].astype(BF16), "w_o": w_o[l].astype(BF16),
            "norm2_g": norm2_g[l].reshape(1, d), "w_router": wr, "b_router": br,
        }
        ssm_mats = _ssm_matrices(ssm_lambda_re[l], ssm_lambda_im[l], ssm_log_dt[l], ssm_b_re[l], ssm_b_im[l],
                                 ssm_c_re[l], ssm_c_im[l], ssm_d[l])
        mods = jnp.split(mod_all[l], 6, axis=-1)
        mods_p = [m[:nb] for m in mods]
        mods_s = [m[nb:n_c] for m in mods]
        prev_s = jnp.pad(cache_pool[l], ((0, 0), (pad_hist, 0), (0, 0)))

        xp_mid, h2_p, lg_p, o_p = _mixers(xp, mods_p, 0, pool_zero, ssm_zero, ssm_zero, None, l, wts, ssm_mats,
                                          lam_init)
        xs_mid, h2_s, lg_s, o_s = _mixers(xs, mods_s, past, prev_s, state_ssm_re[l], state_ssm_im[l],
                                          (cache_k4, cache_v4), l, wts, ssm_mats, lam_init)
        outs_p.append(o_p)
        outs_s.append(o_s)

        h2_all = jnp.concatenate([h2_p, h2_s], axis=0)
        logits = jnp.concatenate([lg_p[:, :n_exp], lg_s[:, :n_exp]], axis=0)
        slot_tok, dest, gates, blk_expert, n_valid = _route(logits, n_exp, moe_tm)
        y_sorted = moe_experts(h2_all[slot_tok], blk_expert, n_valid, w_up[l].astype(BF16), b_up[l],
                               w_down[l].astype(BF16), b_down[l], moe_tm, 512)
        yg_p = y_sorted[dest[:tp]].reshape(tp, TOP_K * d)
        yg_s = y_sorted[dest[tp:]].reshape(ts, TOP_K * d)
        xp = moe_combine(xp_mid, yg_p, gates[:tp], mods_p[5], seq, _pick_tile(tp, 512)).reshape(nb, seq, d)
        xs = moe_combine(xs_mid, yg_s, gates[tp:], mods_s[5], dseq, _pick_tile(ts, 512)).reshape(db, dseq, d)

    stack = lambda outs, i: jnp.stack([o[i] for o in outs])
    return (xp, xs,
            stack(outs_p, 0), stack(outs_p, 1), stack(outs_p, 2), stack(outs_p, 3), stack(outs_p, 4),
            stack(outs_s, 0), stack(outs_s, 1), stack(outs_s, 2), stack(outs_s, 3), stack(outs_s, 4))
```

```python
import functools
import math

import jax
import jax.numpy as jnp
from jax import lax
from jax.experimental import pallas as pl
from jax.experimental.pallas import tpu as pltpu

F32 = jnp.float32
BF16 = jnp.bfloat16
HIGHEST = lax.Precision.HIGHEST

RMS_EPS = 1e-6
NEG_INF = -1e30
CHUNK = 64
POOL_WINDOWS = (2, 4, 8, 16)
POOL_HIST = 15
HIST_ROWS = 16
ATT_HEADS = 8
QK_DIM = 64
V_DIM = 128
ROT_DIM = 16
ROPE_THETA = 500000.0
SSM_GROUP = 16
SSM_STATE = 64
SSM_CHUNK = 16
TOP_K = 4
MOE_ROW_GROUPS = 2
SWIGLU_LIMIT = 7.0
SWIGLU_ALPHA = 1.702
LANES = 128
LOG2E = math.log2(math.e)
FLASH_SUBTILE = 256
MAX_SHIFT_SPAN = 100.0
ROUTER_PAD = 128
VMEM_LIMIT = 52 * 1024 * 1024


def _params(sem, vmem=VMEM_LIMIT):
    return pltpu.CompilerParams(dimension_semantics=sem, vmem_limit_bytes=vmem)


def _mod_operand(arr, tm, rows_per_mod):
    nb, d = arr.shape
    if rows_per_mod % tm == 0:
        return arr.reshape(nb, 1, d), pl.BlockSpec((None, 1, d), lambda i, *_: ((i * tm) // rows_per_mod, 0, 0))
    return jnp.repeat(arr, rows_per_mod, axis=0), pl.BlockSpec((tm, d), lambda i, *_: (i, 0))


def _ada_kernel(c_ref, w_ref, b_ref, o_ref):
    c = c_ref[...]
    s = c * jax.nn.sigmoid(c)
    o_ref[...] = jnp.dot(s.astype(BF16), w_ref[...].astype(BF16), preferred_element_type=F32) + b_ref[...]


def ada_mod(c_pad, w_ada, b_ada):
    depth, d, n = w_ada.shape
    rows = c_pad.shape[0]
    tn = 1024
    return pl.pallas_call(
        _ada_kernel,
        grid=(depth, n // tn),
        in_specs=[pl.BlockSpec((rows, d), lambda l, j: (0, 0)),
                  pl.BlockSpec((None, d, tn), lambda l, j: (l, 0, j)),
                  pl.BlockSpec((None, 1, tn), lambda l, j: (l, 0, j))],
        out_specs=pl.BlockSpec((None, rows, tn), lambda l, j: (l, 0, j)),
        out_shape=jax.ShapeDtypeStruct((depth, rows, n), F32),
        compiler_params=_params(("parallel", "parallel")),
        name="ada_mod",
    )(c_pad, w_ada, b_ada.reshape(depth, 1, n))


def _norm_mm_kernel(x_ref, g_ref, sh_ref, sc_ref, w_ref, b_ref, o_ref, h_scr, *, act):
    @pl.when(pl.program_id(1) == 0)
    def _():
        x = x_ref[...]
        ms = jnp.mean(x * x, axis=-1, keepdims=True)
        y = x * lax.rsqrt(ms + RMS_EPS) * g_ref[...]
        h_scr[...] = (y * (1.0 + sc_ref[...]) + sh_ref[...]).astype(BF16)

    r = jnp.dot(h_scr[...], w_ref[...], preferred_element_type=F32) + b_ref[...]
    if act == "sigmoid":
        r = jax.nn.sigmoid(r)
    o_ref[...] = r.astype(o_ref.dtype)


def norm_matmul(x, gain, shift, scale, w, bias, *, rows_per_mod, act, out_dtype, tm):
    t, d = x.shape
    n = w.shape[1]
    tn = 1024
    sh, sh_spec = _mod_operand(shift, tm, rows_per_mod)
    sc, sc_spec = _mod_operand(scale, tm, rows_per_mod)
    return pl.pallas_call(
        functools.partial(_norm_mm_kernel, act=act),
        grid=(t // tm, n // tn),
        in_specs=[pl.BlockSpec((tm, d), lambda i, j: (i, 0)),
                  pl.BlockSpec((1, d), lambda i, j: (0, 0)),
                  sh_spec, sc_spec,
                  pl.BlockSpec((d, tn), lambda i, j: (0, j)),
                  pl.BlockSpec((1, tn), lambda i, j: (0, j))],
        out_specs=pl.BlockSpec((tm, tn), lambda i, j: (i, j)),
        out_shape=jax.ShapeDtypeStruct((t, n), out_dtype),
        scratch_shapes=[pltpu.VMEM((tm, d), BF16)],
        compiler_params=_params(("parallel", "arbitrary")),
        name="norm_matmul_" + act,
    )(x, gain.reshape(1, d), sh, sc, w, bias.reshape(1, n))


def _rope_tables(pos):
    half = ROT_DIM // 2
    inv_freq = jnp.exp(-math.log(ROPE_THETA) * jnp.arange(half, dtype=F32) * (2.0 / ROT_DIM))
    ang = pos.astype(F32)[:, None] * inv_freq[None, :]
    cos, sin = jnp.cos(ang), jnp.sin(ang)
    lane = jnp.arange(LANES) % QK_DIM
    idx = lane % half
    c_t = jnp.where(lane[None, :] < ROT_DIM, cos[:, idx], 1.0)
    s1_t = jnp.where((lane[None, :] >= half) & (lane[None, :] < ROT_DIM), sin[:, idx], 0.0)
    s2_t = jnp.where(lane[None, :] < half, -sin[:, idx], 0.0)
    return c_t.astype(F32), s1_t.astype(F32), s2_t.astype(F32)


def _qk_prep_kernel(q_ref, k_ref, v_ref, qg_ref, kg_ref, c_ref, s1_ref, s2_ref, gm_ref,
                    q2_ref, kb_ref, vb_ref, krow_ref, vrow_ref):
    c_t, s1_t, s2_t = c_ref[...], s1_ref[...], s2_ref[...]
    gm = gm_ref[...]
    lane = lax.broadcasted_iota(jnp.int32, (1, LANES), 1)
    first_map = lane < QK_DIM

    def norm_rot(x, g):
        ms = jnp.dot(x * x, gm, preferred_element_type=F32, precision=HIGHEST)
        y = x * lax.rsqrt(ms + RMS_EPS) * g
        return y * c_t + pltpu.roll(y, ROT_DIM // 2, 1) * s1_t + pltpu.roll(y, LANES - ROT_DIM // 2, 1) * s2_t

    for h in range(ATT_HEADS):
        sl = slice(h * LANES, (h + 1) * LANES)
        qr = norm_rot(q_ref[:, sl], qg_ref[...]) * (QK_DIM ** -0.5 * LOG2E)
        q2_ref[h, 0] = jnp.where(first_map, qr, 0.0).astype(BF16)
        q2_ref[h, 1] = jnp.where(first_map, 0.0, qr).astype(BF16)
        kr = norm_rot(k_ref[:, sl], kg_ref[...])
        krow_ref[:, sl] = kr
        kb_ref[h] = kr.astype(BF16)
        v = v_ref[:, sl]
        vb_ref[h] = v.astype(BF16)
    vrow_ref[...] = v_ref[...]


def qk_prep(u3, q_norm_g, k_norm_g, pos, tm):
    b, l, _ = u3.shape
    aw = ATT_HEADS * V_DIM
    c_t, s1_t, s2_t = _rope_tables(pos)
    qg = jnp.tile(q_norm_g, 2).reshape(1, LANES)
    kg = jnp.tile(k_norm_g, 2).reshape(1, LANES)
    grp = jnp.arange(LANES) // QK_DIM
    gm = jnp.where(grp[:, None] == grp[None, :], 1.0 / QK_DIM, 0.0).astype(F32)
    tab = pl.BlockSpec((tm, LANES), lambda bi, i: (i, 0))
    vec = pl.BlockSpec((1, LANES), lambda bi, i: (0, 0))
    return pl.pallas_call(
        _qk_prep_kernel,
        grid=(b, l // tm),
        in_specs=[pl.BlockSpec((None, tm, aw), lambda bi, i: (bi, i, 0)),
                  pl.BlockSpec((None, tm, aw), lambda bi, i: (bi, i, 1)),
                  pl.BlockSpec((None, tm, aw), lambda bi, i: (bi, i, 2)),
                  vec, vec, tab, tab, tab,
                  pl.BlockSpec((LANES, LANES), lambda bi, i: (0, 0))],
        out_specs=[pl.BlockSpec((None, ATT_HEADS, 2, tm, LANES), lambda bi, i: (bi, 0, 0, i, 0)),
                   pl.BlockSpec((None, ATT_HEADS, tm, LANES), lambda bi, i: (bi, 0, i, 0)),
                   pl.BlockSpec((None, ATT_HEADS, tm, LANES), lambda bi, i: (bi, 0, i, 0)),
                   pl.BlockSpec((None, tm, aw), lambda bi, i: (bi, i, 0)),
                   pl.BlockSpec((None, tm, aw), lambda bi, i: (bi, i, 0))],
        out_shape=[jax.ShapeDtypeStruct((b, ATT_HEADS, 2, l, LANES), BF16),
                   jax.ShapeDtypeStruct((b, ATT_HEADS, l, LANES), BF16),
                   jax.ShapeDtypeStruct((b, ATT_HEADS, l, LANES), BF16),
                   jax.ShapeDtypeStruct((b, l, aw), F32),
                   jax.ShapeDtypeStruct((b, l, aw), F32)],
        compiler_params=_params(("parallel", "parallel")),
        name="qk_prep",
    )(u3, u3, u3, qg, kg, c_t, s1_t, s2_t, gm)


def _diff_lambda(dl_ref, lam_init):
    dl = dl_ref[...]
    a = jnp.sum(dl[0:1] * dl[1:2], axis=1, keepdims=True)
    b = jnp.sum(dl[2:3] * dl[3:4], axis=1, keepdims=True)
    return jnp.exp(a) - jnp.exp(b) + lam_init


def _diff_finish(o0, o1, dl_ref, sg_ref, lam_init):
    od = o0 - _diff_lambda(dl_ref, lam_init) * o1
    ms = jnp.mean(od * od, axis=-1, keepdims=True)
    return od * lax.rsqrt(ms + RMS_EPS) * sg_ref[...] * (1.0 - lam_init)


def _softmax_step(s, v, m, l, acc):
    m_new = jnp.maximum(m, jnp.max(s, axis=1, keepdims=True))
    alpha = jnp.exp2(m - m_new)
    p = jnp.exp2(s - m_new)
    l_new = alpha * l + jnp.sum(p, axis=1, keepdims=True)
    acc_new = alpha * acc + jnp.dot(p.astype(BF16), v, preferred_element_type=F32)
    return m_new, l_new, acc_new


_NT = (((1,), (1,)), ((), ()))


def _flash_kernel(q_ref, k_ref, v_ref, dl_ref, sg_ref, o_ref, *, tq, tk, lam_init):
    qi = pl.program_id(2)
    qs = (q_ref[0], q_ref[1])
    n_full = (qi * tq) // tk

    def step(off, carries, visible):
        k = k_ref[pl.ds(off, tk), :]
        v = v_ref[pl.ds(off, tk), :]
        scores = [lax.dot_general(q, k, _NT, preferred_element_type=F32) for q in qs]
        out = []
        for s, carry in zip(scores, carries):
            if visible is not None:
                s = jnp.where(visible, s, NEG_INF)
            out.append(_softmax_step(s, v, *carry))
        return tuple(out)

    def body(kb, carries):
        return step(pl.multiple_of(kb * tk, tk), carries, None)

    init = (jnp.full((tq, 1), NEG_INF, F32), jnp.zeros((tq, 1), F32), jnp.zeros((tq, V_DIM), F32))
    carries = lax.fori_loop(0, n_full, body, (init, init))

    off = pl.multiple_of(n_full * tk, tk)
    row = qi * tq + lax.broadcasted_iota(jnp.int32, (tq, tk), 0)
    col = off + lax.broadcasted_iota(jnp.int32, (tq, tk), 1)
    (_, l0, acc0), (_, l1, acc1) = step(off, carries, (col // CHUNK) <= (row // CHUNK))
    o_ref[...] = _diff_finish(acc0 / l0, acc1 / l1, dl_ref, sg_ref, lam_init).astype(o_ref.dtype)


def flash_diff_attn(q2, kb, vb, diff_lambda, subln_g, lam_init, tq):
    b, h, _, l, _ = q2.shape
    tk = min(2 * tq, l)
    return pl.pallas_call(
        functools.partial(_flash_kernel, tq=tq, tk=tk, lam_init=lam_init),
        grid=(b, h, l // tq),
        in_specs=[pl.BlockSpec((None, None, 2, tq, LANES), lambda bi, hi, qi: (bi, hi, 0, qi, 0)),
                  pl.BlockSpec((None, None, l, LANES), lambda bi, hi, qi: (bi, hi, 0, 0)),
                  pl.BlockSpec((None, None, l, LANES), lambda bi, hi, qi: (bi, hi, 0, 0)),
                  pl.BlockSpec((4, QK_DIM), lambda bi, hi, qi: (0, 0)),
                  pl.BlockSpec((1, V_DIM), lambda bi, hi, qi: (0, 0))],
        out_specs=pl.BlockSpec((None, tq, V_DIM), lambda bi, hi, qi: (bi, qi, hi)),
        out_shape=jax.ShapeDtypeStruct((b, l, h * V_DIM), BF16),
        compiler_params=_params(("parallel", "parallel", "arbitrary")),
        name="flash_diff_attn",
    )(q2, kb, vb, diff_lambda, subln_g.reshape(1, V_DIM))


def _shifted_flash_kernel(c_ref, q_ref, k_ref, v_ref, dl_ref, sg_ref, o_ref, acc_ref, l_ref, *, tq, tk, ks, lam_init):
    qi = pl.program_id(2)
    n_full = (qi * tq) // tk
    shift = c_ref[...]
    acc_ref[...] = jnp.zeros_like(acc_ref)
    l_ref[...] = jnp.zeros_like(l_ref)

    def block(off, masked):
        for j in range(tk // ks):
            start = pl.multiple_of(off + j * ks, ks)
            k = k_ref[pl.ds(start, ks), :]
            v = v_ref[pl.ds(start, ks), :]
            if masked:
                row = qi * tq + lax.broadcasted_iota(jnp.int32, (tq, ks), 0)
                col = start + lax.broadcasted_iota(jnp.int32, (tq, ks), 1)
                visible = (col // CHUNK) <= (row // CHUNK)
            for m in range(2):
                s = lax.dot_general(q_ref[m], k, _NT, preferred_element_type=F32)
                p = jnp.exp2(s - shift)
                if masked:
                    p = jnp.where(visible, p, 0.0)
                part = p[:, :LANES]
                for t in range(1, ks // LANES):
                    part = part + p[:, t * LANES:(t + 1) * LANES]
                l_ref[m] += part
                acc_ref[m] += jnp.dot(p.astype(BF16), v, preferred_element_type=F32)

    def body(kb, carry):
        block(kb * tk, False)
        return carry

    lax.fori_loop(0, n_full, body, 0)
    block(n_full * tk, True)
    o0 = acc_ref[0] / jnp.sum(l_ref[0], axis=1, keepdims=True)
    o1 = acc_ref[1] / jnp.sum(l_ref[1], axis=1, keepdims=True)
    o_ref[...] = _diff_finish(o0, o1, dl_ref, sg_ref, lam_init).astype(o_ref.dtype)


def shifted_flash_diff_attn(q2, kb, vb, shift, diff_lambda, subln_g, lam_init, tq):
    b, h, _, l, _ = q2.shape
    tk = min(2 * tq, l)
    ks = min(FLASH_SUBTILE, tk)
    return pl.pallas_call(
        functools.partial(_shifted_flash_kernel, tq=tq, tk=tk, ks=ks, lam_init=lam_init),
        grid=(b, h, l // tq),
        in_specs=[pl.BlockSpec((1, 1), lambda bi, hi, qi: (0, 0)),
                  pl.BlockSpec((None, None, 2, tq, LANES), lambda bi, hi, qi: (bi, hi, 0, qi, 0)),
                  pl.BlockSpec((None, None, l, LANES), lambda bi, hi, qi: (bi, hi, 0, 0)),
                  pl.BlockSpec((None, None, l, LANES), lambda bi, hi, qi: (bi, hi, 0, 0)),
                  pl.BlockSpec((4, QK_DIM), lambda bi, hi, qi: (0, 0)),
                  pl.BlockSpec((1, V_DIM), lambda bi, hi, qi: (0, 0))],
        out_specs=pl.BlockSpec((None, tq, V_DIM), lambda bi, hi, qi: (bi, qi, hi)),
        out_shape=jax.ShapeDtypeStruct((b, l, h * V_DIM), BF16),
        scratch_shapes=[pltpu.VMEM((2, tq, V_DIM), F32), pltpu.VMEM((2, tq, LANES), F32)],
        compiler_params=_params(("parallel", "parallel", "arbitrary")),
        name="shifted_flash_diff_attn",
    )(shift.reshape(1, 1), q2, kb, vb, diff_lambda, subln_g.reshape(1, V_DIM))


def prompt_diff_attn(q2, kb, vb, q_norm_g, k_norm_g, diff_lambda, subln_g, lam_init, tq):
    c = (QK_DIM ** 0.5 * LOG2E) * jnp.max(jnp.abs(q_norm_g)) * jnp.max(jnp.abs(k_norm_g))
    return lax.cond(
        2.0 * c < MAX_SHIFT_SPAN,
        lambda: shifted_flash_diff_attn(q2, kb, vb, c.astype(F32), diff_lambda, subln_g, lam_init, tq),
        lambda: flash_diff_attn(q2, kb, vb, diff_lambda, subln_g, lam_init, tq))


def _cached_attn_kernel(q_ref, ck_ref, cv_ref, k_ref, v_ref, dl_ref, sg_ref, o_ref, *, rows, lam_init):
    q = q_ref[...].reshape(2 * rows, LANES)
    s_c = lax.dot_general(q, ck_ref[...].astype(BF16), _NT, preferred_element_type=F32)
    s_n = lax.dot_general(q, k_ref[...], _NT, preferred_element_type=F32)
    m = jnp.maximum(jnp.max(s_c, axis=1, keepdims=True), jnp.max(s_n, axis=1, keepdims=True))
    p_c = jnp.exp2(s_c - m)
    p_n = jnp.exp2(s_n - m)
    l = jnp.sum(p_c, axis=1, keepdims=True) + jnp.sum(p_n, axis=1, keepdims=True)
    acc = (jnp.dot(p_c.astype(BF16), cv_ref[...].astype(BF16), preferred_element_type=F32)
           + jnp.dot(p_n.astype(BF16), v_ref[...], preferred_element_type=F32))
    o = acc / l
    o_ref[...] = _diff_finish(o[:rows], o[rows:], dl_ref, sg_ref, lam_init).astype(o_ref.dtype)


def cached_diff_attn(q2, kb, vb, cache_k, cache_v, layer, diff_lambda, subln_g, lam_init):
    b, h, _, rows, _ = q2.shape
    past = cache_k.shape[2]
    cache_spec = pl.BlockSpec((None, None, past, LANES), lambda bi, hi: (layer, bi, 0, hi))
    new_spec = pl.BlockSpec((None, None, rows, LANES), lambda bi, hi: (bi, hi, 0, 0))
    return pl.pallas_call(
        functools.partial(_cached_attn_kernel, rows=rows, lam_init=lam_init),
        grid=(b, h),
        in_specs=[pl.BlockSpec((None, None, 2, rows, LANES), lambda bi, hi: (bi, hi, 0, 0, 0)),
                  cache_spec, cache_spec, new_spec, new_spec,
                  pl.BlockSpec((4, QK_DIM), lambda bi, hi: (0, 0)),
                  pl.BlockSpec((1, V_DIM), lambda bi, hi: (0, 0))],
        out_specs=pl.BlockSpec((None, rows, V_DIM), lambda bi, hi: (bi, 0, hi)),
        out_shape=jax.ShapeDtypeStruct((b, rows, h * V_DIM), BF16),
        compiler_params=_params(("parallel", "parallel")),
        name="cached_diff_attn",
    )(q2, cache_k, cache_v, kb, vb, diff_lambda, subln_g.reshape(1, V_DIM))


def _pool_kernel(u_ref, hist_ref, prev_ref, w_ref, sc_ref, o_ref, *, tm, pos0):
    i = pl.program_id(1)
    hist = jnp.where(i == 0, prev_ref[...], hist_ref[...])
    pos = pos0 + i * tm + lax.broadcasted_iota(jnp.int32, (tm, 1), 0)
    gw = LANES
    for g, w in enumerate(POOL_WINDOWS):
        sl = slice(g * gw, (g + 1) * gw)
        ext = jnp.concatenate([hist[:, sl], u_ref[:, sl]], axis=0)
        s = ext
        step = 1
        while step < w:
            s = s + pltpu.roll(s, step, 0)
            step *= 2
        cnt = jnp.minimum(pos + 1, w).astype(F32)
        d = s[HIST_ROWS:] / cnt - ext[HIST_ROWS:]
        y = jnp.dot(d.astype(BF16), w_ref[g], preferred_element_type=F32) * sc_ref[:, sl]
        o_ref[:, sl] = y.astype(o_ref.dtype)


def pool_mix(u3, col_block, prev16, w_mix, pool_scale, pos0, tm):
    b, l, _ = u3.shape
    pw = w_mix.shape[0] * w_mix.shape[1]
    hb = tm // HIST_ROWS
    return pl.pallas_call(
        functools.partial(_pool_kernel, tm=tm, pos0=pos0),
        grid=(b, l // tm),
        in_specs=[pl.BlockSpec((None, tm, pw), lambda bi, i: (bi, i, col_block)),
                  pl.BlockSpec((None, HIST_ROWS, pw), lambda bi, i: (bi, jnp.maximum(i * hb - 1, 0), col_block)),
                  pl.BlockSpec((None, HIST_ROWS, pw), lambda bi, i: (bi, 0, 0)),
                  pl.BlockSpec(w_mix.shape, lambda bi, i: (0, 0, 0)),
                  pl.BlockSpec((1, pw), lambda bi, i: (0, 0))],
        out_specs=pl.BlockSpec((None, tm, pw), lambda bi, i: (bi, i, 0)),
        out_shape=jax.ShapeDtypeStruct((b, l, pw), BF16),
        compiler_params=_params(("parallel", "parallel")),
        name="pool_mix",
    )(u3, u3, prev16, w_mix.astype(BF16), pool_scale.reshape(1, pw))


def _ssm_matrices(lam_re, lam_im, log_dt, b_re, b_im, c_re, c_im, d_skip):
    g, n = lam_re.shape
    cw = SSM_GROUP
    lc = SSM_CHUNK
    dt = jnp.exp(log_dt.astype(F32))[:, None]
    d_idx = jnp.arange(lc + 1, dtype=F32)[None, :, None]
    mag = jnp.exp(lam_re[:, None, :] * dt[:, :, None] * d_idx)
    ang = lam_im[:, None, :] * dt[:, :, None] * d_idx
    pw_re, pw_im = mag * jnp.cos(ang), mag * jnp.sin(ang)
    num_re, num_im = pw_re[:, 1] - 1.0, pw_im[:, 1]
    den = lam_re * lam_re + lam_im * lam_im
    f_re = (num_re * lam_re + num_im * lam_im) / den
    f_im = (num_im * lam_re - num_re * lam_im) / den
    bb_re = f_re[..., None] * b_re - f_im[..., None] * b_im
    bb_im = f_re[..., None] * b_im + f_im[..., None] * b_re
    cp_re = c_re[:, None] * pw_re[:, :, None, :] - c_im[:, None] * pw_im[:, :, None, :]
    cp_im = c_re[:, None] * pw_im[:, :, None, :] + c_im[:, None] * pw_re[:, :, None, :]
    w = (jnp.einsum("gdcn,gne->gdce", cp_re[:, :lc], bb_re, precision=HIGHEST)
         - jnp.einsum("gdcn,gne->gdce", cp_im[:, :lc], bb_im, precision=HIGHEST))
    s_idx = jnp.arange(lc)[:, None]
    t_idx = jnp.arange(lc)[None, :]
    lag = jnp.clip(t_idx - s_idx, 0, lc - 1)
    toe = jnp.where((t_idx >= s_idx)[None, :, :, None, None], w[:, lag], 0.0)
    m = toe.transpose(0, 1, 4, 2, 3).reshape(g, lc * cw, lc * cw)
    dvec = jnp.tile(d_skip.reshape(g, 1, cw), (1, lc, 1)).reshape(g, lc * cw)
    m = m + jnp.eye(lc * cw, dtype=F32)[None] * dvec[:, None, :]
    pr, pi = pw_re[:, lc - 1::-1][:, :lc], pw_im[:, lc - 1::-1][:, :lc]
    p_re = pr[:, :, None, :] * bb_re.transpose(0, 2, 1)[:, None] - pi[:, :, None, :] * bb_im.transpose(0, 2, 1)[:, None]
    p_im = pr[:, :, None, :] * bb_im.transpose(0, 2, 1)[:, None] + pi[:, :, None, :] * bb_re.transpose(0, 2, 1)[:, None]
    p = jnp.concatenate([p_re, p_im], axis=-1).reshape(g, lc * cw, 2 * n)
    q = jnp.concatenate([cp_re[:, 1:].transpose(0, 3, 1, 2), -cp_im[:, 1:].transpose(0, 3, 1, 2)], axis=1)
    q = q.reshape(g, 2 * n, lc * cw)
    a_re, a_im = pw_re[:, lc], pw_im[:, lc]
    a1 = jnp.concatenate([a_re, a_re], axis=-1).reshape(g, 1, 2 * n)
    a2 = jnp.concatenate([-a_im, a_im], axis=-1).reshape(g, 1, 2 * n)
    return m, p, q, a1, a2


def _ssm_s_kernel(u_ref, p_ref, s_ref):
    s_ref[...] = jnp.dot(u_ref[...], p_ref[...], preferred_element_type=F32, precision=HIGHEST)


def _ssm_scan_kernel(s_ref, a1_ref, a2_ref, h0_ref, h_ref, hf_ref, *, nc):
    a1 = a1_ref[...]
    a2 = a2_ref[...]

    def body(c, h):
        h_ref[:, pl.ds(c, 1), :] = h
        s = s_ref[:, pl.ds(c, 1), :]
        return a1 * h + a2 * pltpu.roll(h, SSM_STATE, 2) + s

    hf_ref[...] = lax.fori_loop(0, nc, body, h0_ref[...])


def _ssm_y_kernel(u_ref, h_ref, m_ref, q_ref, y_ref):
    y_ref[...] = (jnp.dot(u_ref[...], m_ref[...], preferred_element_type=F32, precision=HIGHEST)
                  + jnp.dot(h_ref[...], q_ref[...], preferred_element_type=F32, precision=HIGHEST))


def s5_mix(u_ssm, h0_re, h0_im, mats):
    m, p, q, a1, a2 = mats
    b, l, w = u_ssm.shape
    g = w // SSM_GROUP
    n2 = 2 * SSM_STATE
    nc = l // SSM_CHUNK
    rows = b * nc
    kw = SSM_CHUNK * SSM_GROUP
    ug = u_ssm.reshape(rows, SSM_CHUNK, g, SSM_GROUP).transpose(2, 0, 1, 3).reshape(g, rows, kw)
    grp = lambda *shape: pl.BlockSpec((None,) + shape, lambda gi: (gi, 0, 0))
    s = pl.pallas_call(
        _ssm_s_kernel, grid=(g,),
        in_specs=[grp(rows, kw), grp(kw, n2)], out_specs=grp(rows, n2),
        out_shape=jax.ShapeDtypeStruct((g, rows, n2), F32),
        compiler_params=_params(("parallel",)), name="ssm_chunk_state",
    )(ug, p)
    h0 = jnp.concatenate([h0_re, h0_im], axis=-1).transpose(1, 0, 2).reshape(g * b, 1, n2)
    hs, hf = pl.pallas_call(
        functools.partial(_ssm_scan_kernel, nc=nc),
        out_shape=[jax.ShapeDtypeStruct((g * b, nc, n2), F32), jax.ShapeDtypeStruct((g * b, 1, n2), F32)],
        compiler_params=pltpu.CompilerParams(vmem_limit_bytes=VMEM_LIMIT), name="ssm_scan",
    )(s.reshape(g * b, nc, n2), jnp.repeat(a1, b, axis=0), jnp.repeat(a2, b, axis=0), h0)
    y = pl.pallas_call(
        _ssm_y_kernel, grid=(g,),
        in_specs=[grp(rows, kw), grp(rows, n2), grp(kw, kw), grp(n2, kw)], out_specs=grp(rows, kw),
        out_shape=jax.ShapeDtypeStruct((g, rows, kw), F32),
        compiler_params=_params(("parallel",)), name="ssm_chunk_out",
    )(ug, hs.reshape(g, rows, n2), m, q)
    y = y.reshape(g, rows, SSM_CHUNK, SSM_GROUP).transpose(1, 2, 0, 3).reshape(b, l, w)
    hf = hf.reshape(g, b, n2).transpose(1, 0, 2)
    return y, hf[..., :SSM_STATE], hf[..., SSM_STATE:]


def _whole(arr):
    return pl.BlockSpec(arr.shape, lambda i: (0,) * arr.ndim, pipeline_mode=pl.Buffered(1))


def _merge_kernel(pm_ref, oa_ref, ys_ref, g0_ref, g1_ref, g2_ref,
                  wglu_ref, bglu_ref, wpo_ref, wao_ref, wso_ref, o_ref):
    gl = jax.nn.gelu(ys_ref[...])
    z = gl * jax.nn.sigmoid(jnp.dot(gl.astype(BF16), wglu_ref[...], preferred_element_type=F32) + bglu_ref[...])
    merged = g0_ref[...] * jnp.dot(pm_ref[...], wpo_ref[...], preferred_element_type=F32)
    merged = merged + g1_ref[...] * jnp.dot(oa_ref[...], wao_ref[...], preferred_element_type=F32)
    merged = merged + g2_ref[...] * jnp.dot(z.astype(BF16), wso_ref[...], preferred_element_type=F32)
    o_ref[...] = merged.astype(BF16)


def merge_branches(pm, oa, ys, gates, wts, tm):
    t = pm.shape[0]
    d = wts["w_o"].shape[0]

    def rows(width, col=0):
        return pl.BlockSpec((tm, width), lambda i: (i, col))

    consts = [wts["w_glu"], wts["b_glu"], wts["w_pool_out"], wts["w_attn_out"], wts["w_ssm_out"]]
    return pl.pallas_call(
        _merge_kernel,
        grid=(t // tm,),
        in_specs=[rows(pm.shape[1]), rows(oa.shape[1]), rows(ys.shape[1]), rows(d, 0), rows(d, 1), rows(d, 2)]
                 + [_whole(a) for a in consts],
        out_specs=rows(d),
        out_shape=jax.ShapeDtypeStruct((t, d), BF16),
        compiler_params=_params(("parallel",)),
        name="merge_branches",
    )(pm, oa, ys, gates, gates, gates, *consts)


def _out_proj_kernel(m_ref, x_ref, gate1_ref, wo_ref, n2g_ref, sh2_ref, sc2_ref, wr_ref, br_ref,
                     xo_ref, h2_ref, lg_ref):
    x = x_ref[...] + gate1_ref[...] * jnp.dot(m_ref[...], wo_ref[...], preferred_element_type=F32)
    xo_ref[...] = x
    ms = jnp.mean(x * x, axis=-1, keepdims=True)
    h2 = x * lax.rsqrt(ms + RMS_EPS) * n2g_ref[...] * (1.0 + sc2_ref[...]) + sh2_ref[...]
    h2_hi = h2.astype(BF16)
    h2_ref[...] = h2_hi
    h2_lo = (h2 - h2_hi.astype(F32)).astype(BF16)
    w_hi, w_lo = wr_ref[0], wr_ref[1]
    lg_ref[...] = (jnp.dot(h2_hi, w_hi, preferred_element_type=F32) + jnp.dot(h2_hi, w_lo, preferred_element_type=F32)
                   + jnp.dot(h2_lo, w_hi, preferred_element_type=F32) + br_ref[...])


def out_proj(merged, x, gate1, shift2, scale2, wts, rows_per_mod, tm):
    t, d = x.shape
    g1op, g1_spec = _mod_operand(gate1, tm, rows_per_mod)
    sh2, sh2_spec = _mod_operand(shift2, tm, rows_per_mod)
    sc2, sc2_spec = _mod_operand(scale2, tm, rows_per_mod)
    rows = lambda width: pl.BlockSpec((tm, width), lambda i: (i, 0))
    return pl.pallas_call(
        _out_proj_kernel,
        grid=(t // tm,),
        in_specs=[rows(d), rows(d), g1_spec, _whole(wts["w_o"]), _whole(wts["norm2_g"]), sh2_spec, sc2_spec,
                  _whole(wts["w_router"]), _whole(wts["b_router"])],
        out_specs=[rows(d), rows(d), rows(ROUTER_PAD)],
        out_shape=[jax.ShapeDtypeStruct((t, d), F32), jax.ShapeDtypeStruct((t, d), BF16),
                   jax.ShapeDtypeStruct((t, ROUTER_PAD), F32)],
        compiler_params=_params(("parallel",)),
        name="out_proj",
    )(merged, x, g1op, wts["w_o"], wts["norm2_g"], sh2, sc2, wts["w_router"], wts["b_router"])


def _moe_kernel(be_ref, nv_ref, x_ref, wg_ref, wu_ref, bg_ref, bu_ref, wd_ref, bd_ref, o_ref, acc_ref, *, nf):
    i, f = pl.program_id(0), pl.program_id(1)

    @pl.when(i < nv_ref[0])
    def _():
        wg, wu, wd = wg_ref[...], wu_ref[...], wd_ref[...]
        rows = x_ref.shape[0] // MOE_ROW_GROUPS
        for c in range(MOE_ROW_GROUPS):
            rs = pl.ds(c * rows, rows)
            x = x_ref[rs, :]
            hg = jnp.dot(x, wg, preferred_element_type=F32) + bg_ref[...]
            hu = jnp.dot(x, wu, preferred_element_type=F32) + bu_ref[...]
            gate = jnp.minimum(hg, SWIGLU_LIMIT)
            up = jnp.clip(hu, -SWIGLU_LIMIT, SWIGLU_LIMIT)
            act = gate * jax.nn.sigmoid(SWIGLU_ALPHA * gate) * (up + 1.0)
            part = jnp.dot(act.astype(BF16), wd, preferred_element_type=F32)
            acc = jnp.where(f == 0, part, acc_ref[rs, :] + part)
            acc_ref[rs, :] = acc
            o_ref[rs, :] = (acc + bd_ref[...]).astype(o_ref.dtype)


def moe_experts(x_sorted, blk_expert, n_valid, layer, w_up, b_up, w_down, b_down, tm, tf):
    s, d = x_sorted.shape
    depth, e, _, f2 = w_up.shape
    ff = f2 // 2
    nf = ff // tf

    def blk(i, nv):
        return jnp.minimum(i, nv[0] - 1)

    grid_spec = pltpu.PrefetchScalarGridSpec(
        num_scalar_prefetch=2,
        grid=(s // tm, nf),
        in_specs=[pl.BlockSpec((tm, d), lambda i, f, be, nv: (blk(i, nv), 0)),
                  pl.BlockSpec((None, None, d, tf), lambda i, f, be, nv: (layer, be[blk(i, nv)], 0, f)),
                  pl.BlockSpec((None, None, d, tf), lambda i, f, be, nv: (layer, be[blk(i, nv)], 0, nf + f)),
                  pl.BlockSpec((None, None, 1, tf), lambda i, f, be, nv: (layer, be[blk(i, nv)], 0, f)),
                  pl.BlockSpec((None, None, 1, tf), lambda i, f, be, nv: (layer, be[blk(i, nv)], 0, nf + f)),
                  pl.BlockSpec((None, None, tf, d), lambda i, f, be, nv: (layer, be[blk(i, nv)], f, 0)),
                  pl.BlockSpec((None, None, 1, d), lambda i, f, be, nv: (layer, be[blk(i, nv)], 0, 0))],
        out_specs=pl.BlockSpec((tm, d), lambda i, f, be, nv: (blk(i, nv), 0)),
        scratch_shapes=[pltpu.VMEM((tm, d), F32)],
    )
    b_up4 = b_up.reshape(depth, e, 1, f2)
    return pl.pallas_call(
        functools.partial(_moe_kernel, nf=nf),
        grid_spec=grid_spec,
        out_shape=jax.ShapeDtypeStruct((s, d), BF16),
        compiler_params=_params(("arbitrary", "arbitrary")),
        name="moe_experts",
    )(blk_expert, n_valid, x_sorted, w_up, w_up, b_up4, b_up4, w_down, b_down.reshape(depth, e, 1, d))


def _combine_kernel(x_ref, *refs):
    y_refs, (gt_ref, gate2_ref, o_ref) = refs[:TOP_K], refs[TOP_K:]
    gt = gt_ref[...]
    acc = gt[:, 0:1] * y_refs[0][...].astype(F32)
    for k in range(1, TOP_K):
        acc = acc + gt[:, k:k + 1] * y_refs[k][...].astype(F32)
    o_ref[...] = x_ref[...] + gate2_ref[...] * acc


def moe_combine(x, y_gathered, gates, gate2, rows_per_mod, tm):
    t, d = x.shape
    g2op, g2_spec = _mod_operand(gate2, tm, rows_per_mod)
    row_spec = pl.BlockSpec((tm, d), lambda i: (i, 0))
    return pl.pallas_call(
        _combine_kernel,
        grid=(t // tm,),
        in_specs=[row_spec] + [row_spec] * TOP_K + [pl.BlockSpec((tm, TOP_K), lambda i: (i, 0)), g2_spec],
        out_specs=row_spec,
        out_shape=jax.ShapeDtypeStruct((t, d), F32),
        compiler_params=_params(("parallel",)),
        name="moe_combine",
    )(x, *y_gathered, gates, g2op)


def _route(logits, n_experts, tm):
    t = logits.shape[0]
    top_v, top_i = lax.top_k(logits, TOP_K)
    gates = jax.nn.softmax(top_v, axis=-1)
    tk = t * TOP_K
    flat_e = top_i.reshape(tk)
    onehot = (flat_e[:, None] == jnp.arange(n_experts, dtype=flat_e.dtype)[None, :]).astype(jnp.int32)
    csum = jnp.cumsum(onehot, axis=0)
    rank = jnp.take_along_axis(csum, flat_e[:, None], axis=1)[:, 0] - 1
    counts = csum[-1]
    padded = (counts + tm - 1) // tm * tm
    pad_end = jnp.cumsum(padded)
    pad_start = pad_end - padded
    dest = pad_start[flat_e] + rank
    n_blocks = -(-tk // tm) + n_experts
    flat_t = jnp.repeat(jnp.arange(t, dtype=jnp.int32), TOP_K)
    slot_tok = (jnp.arange(n_blocks * tm, dtype=jnp.int32) % t).at[dest].set(flat_t)
    blk_start = jnp.arange(n_blocks, dtype=jnp.int32) * tm
    blk_expert = jnp.minimum(jnp.sum(pad_end[None, :] <= blk_start[:, None], axis=1), n_experts - 1).astype(jnp.int32)
    n_valid = (pad_end[-1] // tm).astype(jnp.int32).reshape(1)
    return slot_tok, dest.reshape(t, TOP_K), gates, blk_expert, n_valid


def _pick_tile(n, pref):
    return pref if n % pref == 0 else n


def _mixers(x, mods, pos0, pool_prev16, h0_re, h0_im, kv_cache, layer, wts, ssm_mats, lam_init):
    b, l, d = x.shape
    t = b * l
    shift1, scale1, gate1, shift2, scale2, _ = mods
    xt = x.reshape(t, d)
    tm = _pick_tile(t, 512)
    tm_in = _pick_tile(t, 1024)
    u = norm_matmul(xt, wts["norm1_g"], shift1, scale1, wts["w_in"], wts["b_in"],
                    rows_per_mod=l, act="none", out_dtype=F32, tm=tm_in)
    gates = norm_matmul(xt, wts["norm1_g"], shift1, scale1, wts["w_gate"], wts["b_gate"],
                        rows_per_mod=l, act="sigmoid", out_dtype=BF16, tm=tm_in)
    u3 = u.reshape(b, l, u.shape[1])
    pos = pos0 + jnp.arange(l, dtype=jnp.int32)
    tl = _pick_tile(l, 512)
    q2, kb, vb, k_rows, v_rows = qk_prep(u3, wts["q_norm_g"], wts["k_norm_g"], pos, tl)
    if kv_cache is None:
        oa = prompt_diff_attn(q2, kb, vb, wts["q_norm_g"], wts["k_norm_g"], wts["diff_lambda"], wts["subln_g"],
                              lam_init, tl)
    else:
        oa = cached_diff_attn(q2, kb, vb, kv_cache[0], kv_cache[1], layer, wts["diff_lambda"], wts["subln_g"],
                              lam_init)
    aw = ATT_HEADS * V_DIM
    pw = wts["pool_scale"].shape[0]
    pm = pool_mix(u3, (3 * aw) // pw, pool_prev16, wts["w_pool_mix"], wts["pool_scale"], pos0, tl)
    pool_new = u3[:, l - POOL_HIST:, 3 * aw:3 * aw + pw]
    ys, ssm_re, ssm_im = s5_mix(u3[:, :, 3 * aw + pw:], h0_re, h0_im, ssm_mats)
    merged = merge_branches(pm.reshape(t, pw), oa.reshape(t, aw), ys.reshape(t, -1), gates, wts, tm)
    x_new, h2, logits = out_proj(merged, xt, gate1, shift2, scale2, wts, l, tm)
    outs = (k_rows.reshape(b, l, ATT_HEADS, 2 * QK_DIM), v_rows.reshape(b, l, ATT_HEADS, V_DIM),
            pool_new, ssm_re, ssm_im)
    return x_new, h2, logits, outs


def kernel(x_prompt, x_sample, c_prompt, c_sample, cache_k, cache_v, cache_pool, state_ssm_re, state_ssm_im, w_ada, b_ada, norm1_g, w_in, w_gate, b_gate, w_pool_mix, pool_scale, w_pool_out, q_norm_g, k_norm_g, diff_lambda, subln_g, w_attn_out, ssm_lambda_re, ssm_lambda_im, ssm_log_dt, ssm_b_re, ssm_b_im, ssm_c_re, ssm_c_im, ssm_d, w_glu, b_glu, w_ssm_out, w_o, norm2_g, w_router, b_router, w_up, b_up, w_down, b_down):
    nb, seq, d = x_prompt.shape
    db, dseq, _ = x_sample.shape
    depth = w_ada.shape[0]
    past = cache_k.shape[2]
    n_exp = w_router.shape[-1]
    aw = ATT_HEADS * V_DIM
    pw = pool_scale.shape[-1]
    tp, ts = nb * seq, db * dseq

    n_c = nb + db
    c_rows = -(-n_c // 8) * 8
    c_pad = jnp.zeros((c_rows, d), F32).at[:n_c].set(jnp.concatenate([c_prompt, c_sample], axis=0))
    mod_all = ada_mod(c_pad, w_ada, b_ada)

    cache_k4 = cache_k.reshape(depth, db, past, aw)
    cache_v4 = cache_v.reshape(depth, db, past, aw)
    pool_zero = jnp.zeros((nb, HIST_ROWS, pw), F32)
    ssm_zero = jnp.zeros((nb,) + state_ssm_re.shape[2:], F32)
    pad_hist = HIST_ROWS - POOL_HIST
    moe_tm, moe_tf = 512, 512
    w_up_h, w_down_h = w_up.astype(BF16), w_down.astype(BF16)

    xp, xs = x_prompt, x_sample
    outs_p, outs_s = [], []
    for l in range(depth):
        lam_init = 0.8 - 0.6 * math.exp(-0.3 * l)
        wi = w_in[l]
        w_in_r = jnp.concatenate([wi[:, pw:pw + 3 * aw], wi[:, :pw], wi[:, pw + 3 * aw:]], axis=1).astype(BF16)
        wr = jnp.zeros((d, ROUTER_PAD), F32).at[:, :n_exp].set(w_router[l])
        wr_hi = wr.astype(BF16)
        wr = jnp.stack([wr_hi, (wr - wr_hi.astype(F32)).astype(BF16)])
        br = jnp.zeros((1, ROUTER_PAD), F32).at[0, :n_exp].set(b_router[l])
        wts = {
            "norm1_g": norm1_g[l], "w_in": w_in_r, "b_in": jnp.zeros((w_in_r.shape[1],), F32),
            "w_gate": w_gate[l].astype(BF16), "b_gate": b_gate[l],
            "w_pool_mix": w_pool_mix[l], "pool_scale": pool_scale[l],
            "q_norm_g": q_norm_g[l], "k_norm_g": k_norm_g[l], "diff_lambda": diff_lambda[l], "subln_g": subln_g[l],
            "w_glu": w_glu[l].astype(BF16), "b_glu": b_glu[l].reshape(1, -1),
            "w_pool_out": w_pool_out[l].astype(BF16), "w_attn_out": w_attn_out[l].astype(BF16),
            "w_ssm_out": w_ssm_out[l].astype(BF16), "w_o": w_o[l].astype(BF16),
            "norm2_g": norm2_g[l].reshape(1, d), "w_router": wr, "b_router": br,
        }
        ssm_mats = _ssm_matrices(ssm_lambda_re[l], ssm_lambda_im[l], ssm_log_dt[l], ssm_b_re[l], ssm_b_im[l],
                                 ssm_c_re[l], ssm_c_im[l], ssm_d[l])
        mods = jnp.split(mod_all[l], 6, axis=-1)
        mods_p = [m[:nb] for m in mods]
        mods_s = [m[nb:n_c] for m in mods]
        prev_s = jnp.pad(cache_pool[l], ((0, 0), (pad_hist, 0), (0, 0)))

        xp_mid, h2_p, lg_p, o_p = _mixers(xp, mods_p, 0, pool_zero, ssm_zero, ssm_zero, None, l, wts, ssm_mats,
                                          lam_init)
        xs_mid, h2_s, lg_s, o_s = _mixers(xs, mods_s, past, prev_s, state_ssm_re[l], state_ssm_im[l],
                                          (cache_k4, cache_v4), l, wts, ssm_mats, lam_init)
        outs_p.append(o_p)
        outs_s.append(o_s)

        h2_all = jnp.concatenate([h2_p, h2_s], axis=0)
        logits = jnp.concatenate([lg_p[:, :n_exp], lg_s[:, :n_exp]], axis=0)
        slot_tok, dest, gates, blk_expert, n_valid = _route(logits, n_exp, moe_tm)
        y_sorted = moe_experts(h2_all[slot_tok], blk_expert, n_valid, l, w_up_h, b_up, w_down_h, b_down, moe_tm,
                               moe_tf)
        yg_p = [y_sorted[dest[:tp, k]] for k in range(TOP_K)]
        yg_s = [y_sorted[dest[tp:, k]] for k in range(TOP_K)]
        xp = moe_combine(xp_mid, yg_p, gates[:tp], mods_p[5], seq, _pick_tile(tp, 512)).reshape(nb, seq, d)
        xs = moe_combine(xs_mid, yg_s, gates[tp:], mods_s[5], dseq, _pick_tile(ts, 512)).reshape(db, dseq, d)

    stack = lambda outs, i: jnp.stack([o[i] for o in outs])
    return (xp, xs,
            stack(outs_p, 0), stack(outs_p, 1), stack(outs_p, 2), stack(outs_p, 3), stack(outs_p, 4),
            stack(outs_s, 0), stack(outs_s, 1), stack(outs_s, 2), stack(outs_s, 3), stack(outs_s, 4))
```

```python
import functools
import math

import jax
import jax.numpy as jnp
from jax import lax
from jax.experimental import pallas as pl
from jax.experimental.pallas import tpu as pltpu

F32 = jnp.float32
BF16 = jnp.bfloat16
HIGHEST = lax.Precision.HIGHEST

RMS_EPS = 1e-6
NEG_INF = -1e30
CHUNK = 64
POOL_WINDOWS = (2, 4, 8, 16)
POOL_HIST = 15
HIST_ROWS = 16
ATT_HEADS = 8
QK_DIM = 64
V_DIM = 128
ROT_DIM = 16
ROPE_THETA = 500000.0
SSM_GROUP = 16
SSM_STATE = 64
SSM_CHUNK = 16
TOP_K = 4
MOE_ROW_GROUPS = 2
SWIGLU_LIMIT = 7.0
SWIGLU_ALPHA = 1.702
LANES = 128
LOG2E = math.log2(math.e)
FLASH_SUBTILE = 256
MAX_SHIFT_SPAN = 100.0
ROUTER_PAD = 128
VMEM_LIMIT = 52 * 1024 * 1024


def _params(sem, vmem=VMEM_LIMIT):
    return pltpu.CompilerParams(dimension_semantics=sem, vmem_limit_bytes=vmem)


def _mod_operand(arr, tm, rows_per_mod):
    nb, d = arr.shape
    if rows_per_mod % tm == 0:
        return arr.reshape(nb, 1, d), pl.BlockSpec((None, 1, d), lambda i, *_: ((i * tm) // rows_per_mod, 0, 0))
    return jnp.repeat(arr, rows_per_mod, axis=0), pl.BlockSpec((tm, d), lambda i, *_: (i, 0))


def _ada_kernel(c_ref, w_ref, b_ref, o_ref):
    c = c_ref[...]
    s = c * jax.nn.sigmoid(c)
    o_ref[...] = jnp.dot(s.astype(BF16), w_ref[...].astype(BF16), preferred_element_type=F32) + b_ref[...]


def ada_mod(c_pad, w_ada, b_ada):
    depth, d, n = w_ada.shape
    rows = c_pad.shape[0]
    tn = 1024
    return pl.pallas_call(
        _ada_kernel,
        grid=(depth, n // tn),
        in_specs=[pl.BlockSpec((rows, d), lambda l, j: (0, 0)),
                  pl.BlockSpec((None, d, tn), lambda l, j: (l, 0, j)),
                  pl.BlockSpec((None, 1, tn), lambda l, j: (l, 0, j))],
        out_specs=pl.BlockSpec((None, rows, tn), lambda l, j: (l, 0, j)),
        out_shape=jax.ShapeDtypeStruct((depth, rows, n), F32),
        compiler_params=_params(("parallel", "parallel")),
        name="ada_mod",
    )(c_pad, w_ada, b_ada.reshape(depth, 1, n))


def _norm_mm_kernel(x_ref, g_ref, sh_ref, sc_ref, w_ref, b_ref, o_ref, h_scr, *, act):
    @pl.when(pl.program_id(1) == 0)
    def _():
        x = x_ref[...]
        ms = jnp.mean(x * x, axis=-1, keepdims=True)
        y = x * lax.rsqrt(ms + RMS_EPS) * g_ref[...]
        h_scr[...] = (y * (1.0 + sc_ref[...]) + sh_ref[...]).astype(BF16)

    w = w_ref[...]
    groups = 2 if h_scr.shape[0] % 32 == 0 else 1
    rows = h_scr.shape[0] // groups
    for c in range(groups):
        rs = pl.ds(c * rows, rows)
        r = jnp.dot(h_scr[rs, :], w, preferred_element_type=F32) + b_ref[...]
        if act == "sigmoid":
            r = jax.nn.sigmoid(r)
        o_ref[rs, :] = r.astype(o_ref.dtype)


def norm_matmul(x, gain, shift, scale, w, bias, *, rows_per_mod, act, out_dtype, tm):
    t, d = x.shape
    n = w.shape[1]
    tn = 1024
    sh, sh_spec = _mod_operand(shift, tm, rows_per_mod)
    sc, sc_spec = _mod_operand(scale, tm, rows_per_mod)
    return pl.pallas_call(
        functools.partial(_norm_mm_kernel, act=act),
        grid=(t // tm, n // tn),
        in_specs=[pl.BlockSpec((tm, d), lambda i, j: (i, 0)),
                  pl.BlockSpec((1, d), lambda i, j: (0, 0)),
                  sh_spec, sc_spec,
                  pl.BlockSpec((d, tn), lambda i, j: (0, j)),
                  pl.BlockSpec((1, tn), lambda i, j: (0, j))],
        out_specs=pl.BlockSpec((tm, tn), lambda i, j: (i, j)),
        out_shape=jax.ShapeDtypeStruct((t, n), out_dtype),
        scratch_shapes=[pltpu.VMEM((tm, d), BF16)],
        compiler_params=_params(("parallel", "arbitrary")),
        name="norm_matmul_" + act,
    )(x, gain.reshape(1, d), sh, sc, w, bias.reshape(1, n))


def _rope_tables(pos):
    half = ROT_DIM // 2
    inv_freq = jnp.exp(-math.log(ROPE_THETA) * jnp.arange(half, dtype=F32) * (2.0 / ROT_DIM))
    ang = pos.astype(F32)[:, None] * inv_freq[None, :]
    cos, sin = jnp.cos(ang), jnp.sin(ang)
    lane = jnp.arange(LANES) % QK_DIM
    idx = lane % half
    c_t = jnp.where(lane[None, :] < ROT_DIM, cos[:, idx], 1.0)
    s1_t = jnp.where((lane[None, :] >= half) & (lane[None, :] < ROT_DIM), sin[:, idx], 0.0)
    s2_t = jnp.where(lane[None, :] < half, -sin[:, idx], 0.0)
    return c_t.astype(F32), s1_t.astype(F32), s2_t.astype(F32)


def _qk_prep_kernel(q_ref, k_ref, v_ref, qg_ref, kg_ref, c_ref, s1_ref, s2_ref, gm_ref,
                    q2_ref, kb_ref, vb_ref, krow_ref, vrow_ref):
    c_t, s1_t, s2_t = c_ref[...], s1_ref[...], s2_ref[...]
    gm = gm_ref[...]
    lane = lax.broadcasted_iota(jnp.int32, (1, LANES), 1)
    first_map = lane < QK_DIM

    def norm_rot(x, g):
        ms = jnp.dot(x * x, gm, preferred_element_type=F32, precision=HIGHEST)
        y = x * lax.rsqrt(ms + RMS_EPS) * g
        return y * c_t + pltpu.roll(y, ROT_DIM // 2, 1) * s1_t + pltpu.roll(y, LANES - ROT_DIM // 2, 1) * s2_t

    for h in range(ATT_HEADS):
        sl = slice(h * LANES, (h + 1) * LANES)
        qr = norm_rot(q_ref[:, sl], qg_ref[...]) * (QK_DIM ** -0.5 * LOG2E)
        q2_ref[h, 0] = jnp.where(first_map, qr, 0.0).astype(BF16)
        q2_ref[h, 1] = jnp.where(first_map, 0.0, qr).astype(BF16)
        kr = norm_rot(k_ref[:, sl], kg_ref[...])
        krow_ref[:, sl] = kr
        kb_ref[h] = kr.astype(BF16)
        v = v_ref[:, sl]
        vb_ref[h] = v.astype(BF16)
    vrow_ref[...] = v_ref[...]


def qk_prep(u3, q_norm_g, k_norm_g, pos, tm):
    b, l, _ = u3.shape
    aw = ATT_HEADS * V_DIM
    c_t, s1_t, s2_t = _rope_tables(pos)
    qg = jnp.tile(q_norm_g, 2).reshape(1, LANES)
    kg = jnp.tile(k_norm_g, 2).reshape(1, LANES)
    grp = jnp.arange(LANES) // QK_DIM
    gm = jnp.where(grp[:, None] == grp[None, :], 1.0 / QK_DIM, 0.0).astype(F32)
    tab = pl.BlockSpec((tm, LANES), lambda bi, i: (i, 0))
    vec = pl.BlockSpec((1, LANES), lambda bi, i: (0, 0))
    return pl.pallas_call(
        _qk_prep_kernel,
        grid=(b, l // tm),
        in_specs=[pl.BlockSpec((None, tm, aw), lambda bi, i: (bi, i, 0)),
                  pl.BlockSpec((None, tm, aw), lambda bi, i: (bi, i, 1)),
                  pl.BlockSpec((None, tm, aw), lambda bi, i: (bi, i, 2)),
                  vec, vec, tab, tab, tab,
                  pl.BlockSpec((LANES, LANES), lambda bi, i: (0, 0))],
        out_specs=[pl.BlockSpec((None, ATT_HEADS, 2, tm, LANES), lambda bi, i: (bi, 0, 0, i, 0)),
                   pl.BlockSpec((None, ATT_HEADS, tm, LANES), lambda bi, i: (bi, 0, i, 0)),
                   pl.BlockSpec((None, ATT_HEADS, tm, LANES), lambda bi, i: (bi, 0, i, 0)),
                   pl.BlockSpec((None, tm, aw), lambda bi, i: (bi, i, 0)),
                   pl.BlockSpec((None, tm, aw), lambda bi, i: (bi, i, 0))],
        out_shape=[jax.ShapeDtypeStruct((b, ATT_HEADS, 2, l, LANES), BF16),
                   jax.ShapeDtypeStruct((b, ATT_HEADS, l, LANES), BF16),
                   jax.ShapeDtypeStruct((b, ATT_HEADS, l, LANES), BF16),
                   jax.ShapeDtypeStruct((b, l, aw), F32),
                   jax.ShapeDtypeStruct((b, l, aw), F32)],
        compiler_params=_params(("parallel", "parallel")),
        name="qk_prep",
    )(u3, u3, u3, qg, kg, c_t, s1_t, s2_t, gm)


def _diff_lambda(dl_ref, lam_init):
    dl = dl_ref[...]
    a = jnp.sum(dl[0:1] * dl[1:2], axis=1, keepdims=True)
    b = jnp.sum(dl[2:3] * dl[3:4], axis=1, keepdims=True)
    return jnp.exp(a) - jnp.exp(b) + lam_init


def _diff_finish(o0, o1, dl_ref, sg_ref, lam_init):
    od = o0 - _diff_lambda(dl_ref, lam_init) * o1
    ms = jnp.mean(od * od, axis=-1, keepdims=True)
    return od * lax.rsqrt(ms + RMS_EPS) * sg_ref[...] * (1.0 - lam_init)


def _softmax_step(s, v, m, l, acc):
    m_new = jnp.maximum(m, jnp.max(s, axis=1, keepdims=True))
    alpha = jnp.exp2(m - m_new)
    p = jnp.exp2(s - m_new)
    l_new = alpha * l + jnp.sum(p, axis=1, keepdims=True)
    acc_new = alpha * acc + jnp.dot(p.astype(BF16), v, preferred_element_type=F32)
    return m_new, l_new, acc_new


_NT = (((1,), (1,)), ((), ()))


def _cast_rows(wu_ref, wd_ref, wuo_ref, wdo_ref):
    wuo_ref[...] = wu_ref[...].astype(BF16)
    wdo_ref[...] = wd_ref[...].astype(BF16)


def _flash_kernel(q_ref, k_ref, v_ref, dl_ref, sg_ref, wu_ref, wd_ref, o_ref, wuo_ref, wdo_ref, *, tq, tk, lam_init):
    _cast_rows(wu_ref, wd_ref, wuo_ref, wdo_ref)
    qi = pl.program_id(2)
    qs = (q_ref[0], q_ref[1])
    n_full = (qi * tq) // tk

    def step(off, carries, visible):
        k = k_ref[pl.ds(off, tk), :]
        v = v_ref[pl.ds(off, tk), :]
        scores = [lax.dot_general(q, k, _NT, preferred_element_type=F32) for q in qs]
        out = []
        for s, carry in zip(scores, carries):
            if visible is not None:
                s = jnp.where(visible, s, NEG_INF)
            out.append(_softmax_step(s, v, *carry))
        return tuple(out)

    def body(kb, carries):
        return step(pl.multiple_of(kb * tk, tk), carries, None)

    init = (jnp.full((tq, 1), NEG_INF, F32), jnp.zeros((tq, 1), F32), jnp.zeros((tq, V_DIM), F32))
    carries = lax.fori_loop(0, n_full, body, (init, init))

    off = pl.multiple_of(n_full * tk, tk)
    row = qi * tq + lax.broadcasted_iota(jnp.int32, (tq, tk), 0)
    col = off + lax.broadcasted_iota(jnp.int32, (tq, tk), 1)
    (_, l0, acc0), (_, l1, acc1) = step(off, carries, (col // CHUNK) <= (row // CHUNK))
    o_ref[...] = _diff_finish(acc0 / l0, acc1 / l1, dl_ref, sg_ref, lam_init).astype(o_ref.dtype)


def _shifted_flash_kernel(c_ref, q_ref, k_ref, v_ref, dl_ref, sg_ref, wu_ref, wd_ref, o_ref, wuo_ref, wdo_ref,
                          acc_ref, l_ref, *, tq, tk, ks, lam_init):
    _cast_rows(wu_ref, wd_ref, wuo_ref, wdo_ref)
    qi = pl.program_id(2)
    n_full = (qi * tq) // tk
    shift = c_ref[...]
    acc_ref[...] = jnp.zeros_like(acc_ref)
    l_ref[...] = jnp.zeros_like(l_ref)

    def sub_tile(start, masked):
        k = k_ref[pl.ds(start, ks), :]
        v = v_ref[pl.ds(start, ks), :]
        if masked:
            row = qi * tq + lax.broadcasted_iota(jnp.int32, (tq, ks), 0)
            col = start + lax.broadcasted_iota(jnp.int32, (tq, ks), 1)
            visible = (col // CHUNK) <= (row // CHUNK)
        for m in range(2):
            s = lax.dot_general(q_ref[m], k, _NT, preferred_element_type=F32)
            p = jnp.exp2(s - shift)
            if masked:
                p = jnp.where(visible, p, 0.0)
            part = p[:, :LANES]
            for t in range(1, ks // LANES):
                part = part + p[:, t * LANES:(t + 1) * LANES]
            l_ref[m] += part
            acc_ref[m] += jnp.dot(p.astype(BF16), v, preferred_element_type=F32)

    def block(kb, masked):
        for j in range(tk // ks):
            sub_tile(pl.multiple_of(kb * tk + j * ks, ks), masked)

    def full_block(kb, carry):
        block(kb, False)
        return carry

    lax.fori_loop(0, n_full, full_block, 0)
    block(n_full, True)
    o0 = acc_ref[0] / jnp.sum(l_ref[0], axis=1, keepdims=True)
    o1 = acc_ref[1] / jnp.sum(l_ref[1], axis=1, keepdims=True)
    o_ref[...] = _diff_finish(o0, o1, dl_ref, sg_ref, lam_init).astype(o_ref.dtype)


def _prompt_attn_call(kernel_fn, name, lead, scratch, q2, kb, vb, diff_lambda, subln_g, w_up, w_down, layer, tq):
    b, h, _, l, _ = q2.shape
    nq = l // tq
    _, e, d_in, f2 = w_up.shape
    _, _, d_ff, d_out = w_down.shape
    steps = b * h * nq
    ru, rd = e * d_in // steps, e * d_ff // steps
    assert ru * steps == e * d_in and d_in % ru == 0 and rd * steps == e * d_ff and d_ff % rd == 0

    def chunk(rows_per_expert, rows):
        per = rows_per_expert // rows

        def idx(bi, hi, qi):
            c = (bi * h + hi) * nq + qi
            return c // per, c % per
        return idx

    iu, idn = chunk(d_in, ru), chunk(d_ff, rd)
    const = lambda shape: pl.BlockSpec(shape, lambda bi, hi, qi: (0,) * len(shape))
    in_specs = ([const(x.shape) for x in lead]
                + [pl.BlockSpec((None, None, 2, tq, LANES), lambda bi, hi, qi: (bi, hi, 0, qi, 0)),
                   pl.BlockSpec((None, None, l, LANES), lambda bi, hi, qi: (bi, hi, 0, 0)),
                   pl.BlockSpec((None, None, l, LANES), lambda bi, hi, qi: (bi, hi, 0, 0)),
                   const((4, QK_DIM)), const((1, V_DIM)),
                   pl.BlockSpec((None, None, ru, f2), lambda bi, hi, qi: (layer,) + iu(bi, hi, qi) + (0,)),
                   pl.BlockSpec((None, None, rd, d_out), lambda bi, hi, qi: (layer,) + idn(bi, hi, qi) + (0,))])
    out_specs = [pl.BlockSpec((None, tq, V_DIM), lambda bi, hi, qi: (bi, qi, hi)),
                 pl.BlockSpec((None, ru, f2), lambda bi, hi, qi: iu(bi, hi, qi) + (0,)),
                 pl.BlockSpec((None, rd, d_out), lambda bi, hi, qi: idn(bi, hi, qi) + (0,))]
    return pl.pallas_call(
        kernel_fn,
        grid=(b, h, nq),
        in_specs=in_specs,
        out_specs=out_specs,
        out_shape=[jax.ShapeDtypeStruct((b, l, h * V_DIM), BF16),
                   jax.ShapeDtypeStruct((e, d_in, f2), BF16), jax.ShapeDtypeStruct((e, d_ff, d_out), BF16)],
        scratch_shapes=scratch,
        compiler_params=_params(("parallel", "parallel", "arbitrary")),
        name=name,
    )(*lead, q2, kb, vb, diff_lambda, subln_g.reshape(1, V_DIM), w_up, w_down)


def prompt_diff_attn(q2, kb, vb, q_norm_g, k_norm_g, diff_lambda, subln_g, w_up, w_down, layer, lam_init, tq):
    l = q2.shape[3]
    tk = min(2 * tq, l)
    ks = min(FLASH_SUBTILE, tk)
    c = ((QK_DIM ** 0.5 * LOG2E) * jnp.max(jnp.abs(q_norm_g)) * jnp.max(jnp.abs(k_norm_g))).astype(F32)
    args = (q2, kb, vb, diff_lambda, subln_g, w_up, w_down, layer, tq)
    return lax.cond(
        2.0 * c < MAX_SHIFT_SPAN,
        lambda: _prompt_attn_call(
            functools.partial(_shifted_flash_kernel, tq=tq, tk=tk, ks=ks, lam_init=lam_init), "shifted_flash_diff_attn",
            [c.reshape(1, 1)], [pltpu.VMEM((2, tq, V_DIM), F32), pltpu.VMEM((2, tq, LANES), F32)], *args),
        lambda: _prompt_attn_call(
            functools.partial(_flash_kernel, tq=tq, tk=tk, lam_init=lam_init), "flash_diff_attn", [], [], *args))


def _cached_attn_kernel(q_ref, ck_ref, cv_ref, k_ref, v_ref, dl_ref, sg_ref, o_ref, *, rows, lam_init):
    q = q_ref[...].reshape(2 * rows, LANES)
    s_c = lax.dot_general(q, ck_ref[...].astype(BF16), _NT, preferred_element_type=F32)
    s_n = lax.dot_general(q, k_ref[...], _NT, preferred_element_type=F32)
    m = jnp.maximum(jnp.max(s_c, axis=1, keepdims=True), jnp.max(s_n, axis=1, keepdims=True))
    p_c = jnp.exp2(s_c - m)
    p_n = jnp.exp2(s_n - m)
    l = jnp.sum(p_c, axis=1, keepdims=True) + jnp.sum(p_n, axis=1, keepdims=True)
    acc = (jnp.dot(p_c.astype(BF16), cv_ref[...].astype(BF16), preferred_element_type=F32)
           + jnp.dot(p_n.astype(BF16), v_ref[...], preferred_element_type=F32))
    o = acc / l
    o_ref[...] = _diff_finish(o[:rows], o[rows:], dl_ref, sg_ref, lam_init).astype(o_ref.dtype)


def cached_diff_attn(q2, kb, vb, cache_k, cache_v, layer, diff_lambda, subln_g, lam_init):
    b, h, _, rows, _ = q2.shape
    past = cache_k.shape[2]
    cache_spec = pl.BlockSpec((None, None, past, LANES), lambda bi, hi: (layer, bi, 0, hi))
    new_spec = pl.BlockSpec((None, None, rows, LANES), lambda bi, hi: (bi, hi, 0, 0))
    return pl.pallas_call(
        functools.partial(_cached_attn_kernel, rows=rows, lam_init=lam_init),
        grid=(b, h),
        in_specs=[pl.BlockSpec((None, None, 2, rows, LANES), lambda bi, hi: (bi, hi, 0, 0, 0)),
                  cache_spec, cache_spec, new_spec, new_spec,
                  pl.BlockSpec((4, QK_DIM), lambda bi, hi: (0, 0)),
                  pl.BlockSpec((1, V_DIM), lambda bi, hi: (0, 0))],
        out_specs=pl.BlockSpec((None, rows, V_DIM), lambda bi, hi: (bi, 0, hi)),
        out_shape=jax.ShapeDtypeStruct((b, rows, h * V_DIM), BF16),
        compiler_params=_params(("parallel", "parallel")),
        name="cached_diff_attn",
    )(q2, cache_k, cache_v, kb, vb, diff_lambda, subln_g.reshape(1, V_DIM))


def _pool_kernel(u_ref, hist_ref, prev_ref, w_ref, sc_ref, o_ref, *, tm, pos0):
    i = pl.program_id(1)
    hist = jnp.where(i == 0, prev_ref[...], hist_ref[...])
    pos = pos0 + i * tm + lax.broadcasted_iota(jnp.int32, (tm, 1), 0)
    gw = LANES
    for g, w in enumerate(POOL_WINDOWS):
        sl = slice(g * gw, (g + 1) * gw)
        ext = jnp.concatenate([hist[:, sl], u_ref[:, sl]], axis=0)
        s = ext
        step = 1
        while step < w:
            s = s + pltpu.roll(s, step, 0)
            step *= 2
        cnt = jnp.minimum(pos + 1, w).astype(F32)
        d = s[HIST_ROWS:] / cnt - ext[HIST_ROWS:]
        y = jnp.dot(d.astype(BF16), w_ref[g], preferred_element_type=F32) * sc_ref[:, sl]
        o_ref[:, sl] = y.astype(o_ref.dtype)


def pool_mix(u3, col_block, prev16, w_mix, pool_scale, pos0, tm):
    b, l, _ = u3.shape
    pw = w_mix.shape[0] * w_mix.shape[1]
    hb = tm // HIST_ROWS
    return pl.pallas_call(
        functools.partial(_pool_kernel, tm=tm, pos0=pos0),
        grid=(b, l // tm),
        in_specs=[pl.BlockSpec((None, tm, pw), lambda bi, i: (bi, i, col_block)),
                  pl.BlockSpec((None, HIST_ROWS, pw), lambda bi, i: (bi, jnp.maximum(i * hb - 1, 0), col_block)),
                  pl.BlockSpec((None, HIST_ROWS, pw), lambda bi, i: (bi, 0, 0)),
                  pl.BlockSpec(w_mix.shape, lambda bi, i: (0, 0, 0)),
                  pl.BlockSpec((1, pw), lambda bi, i: (0, 0))],
        out_specs=pl.BlockSpec((None, tm, pw), lambda bi, i: (bi, i, 0)),
        out_shape=jax.ShapeDtypeStruct((b, l, pw), BF16),
        compiler_params=_params(("parallel", "parallel")),
        name="pool_mix",
    )(u3, u3, prev16, w_mix.astype(BF16), pool_scale.reshape(1, pw))


def _ssm_matrices(lam_re, lam_im, log_dt, b_re, b_im, c_re, c_im, d_skip):
    g, n = lam_re.shape
    cw = SSM_GROUP
    lc = SSM_CHUNK
    dt = jnp.exp(log_dt.astype(F32))[:, None]
    d_idx = jnp.arange(lc + 1, dtype=F32)[None, :, None]
    mag = jnp.exp(lam_re[:, None, :] * dt[:, :, None] * d_idx)
    ang = lam_im[:, None, :] * dt[:, :, None] * d_idx
    pw_re, pw_im = mag * jnp.cos(ang), mag * jnp.sin(ang)
    num_re, num_im = pw_re[:, 1] - 1.0, pw_im[:, 1]
    den = lam_re * lam_re + lam_im * lam_im
    f_re = (num_re * lam_re + num_im * lam_im) / den
    f_im = (num_im * lam_re - num_re * lam_im) / den
    bb_re = f_re[..., None] * b_re - f_im[..., None] * b_im
    bb_im = f_re[..., None] * b_im + f_im[..., None] * b_re
    cp_re = c_re[:, None] * pw_re[:, :, None, :] - c_im[:, None] * pw_im[:, :, None, :]
    cp_im = c_re[:, None] * pw_im[:, :, None, :] + c_im[:, None] * pw_re[:, :, None, :]
    w = (jnp.einsum("gdcn,gne->gdce", cp_re[:, :lc], bb_re, precision=HIGHEST)
         - jnp.einsum("gdcn,gne->gdce", cp_im[:, :lc], bb_im, precision=HIGHEST))
    s_idx = jnp.arange(lc)[:, None]
    t_idx = jnp.arange(lc)[None, :]
    lag = jnp.clip(t_idx - s_idx, 0, lc - 1)
    toe = jnp.where((t_idx >= s_idx)[None, :, :, None, None], w[:, lag], 0.0)
    m = toe.transpose(0, 1, 4, 2, 3).reshape(g, lc * cw, lc * cw)
    dvec = jnp.tile(d_skip.reshape(g, 1, cw), (1, lc, 1)).reshape(g, lc * cw)
    m = m + jnp.eye(lc * cw, dtype=F32)[None] * dvec[:, None, :]
    pr, pi = pw_re[:, lc - 1::-1][:, :lc], pw_im[:, lc - 1::-1][:, :lc]
    p_re = pr[:, :, None, :] * bb_re.transpose(0, 2, 1)[:, None] - pi[:, :, None, :] * bb_im.transpose(0, 2, 1)[:, None]
    p_im = pr[:, :, None, :] * bb_im.transpose(0, 2, 1)[:, None] + pi[:, :, None, :] * bb_re.transpose(0, 2, 1)[:, None]
    p = jnp.concatenate([p_re, p_im], axis=-1).reshape(g, lc * cw, 2 * n)
    q = jnp.concatenate([cp_re[:, 1:].transpose(0, 3, 1, 2), -cp_im[:, 1:].transpose(0, 3, 1, 2)], axis=1)
    q = q.reshape(g, 2 * n, lc * cw)
    a_re, a_im = pw_re[:, lc], pw_im[:, lc]
    a1 = jnp.concatenate([a_re, a_re], axis=-1).reshape(g, 1, 2 * n)
    a2 = jnp.concatenate([-a_im, a_im], axis=-1).reshape(g, 1, 2 * n)
    return m, p, q, a1, a2


def _ssm_s_kernel(u_ref, p_ref, s_ref):
    s_ref[...] = jnp.dot(u_ref[...], p_ref[...], preferred_element_type=F32, precision=HIGHEST)


def _ssm_scan_kernel(s_ref, a1_ref, a2_ref, h0_ref, h_ref, hf_ref, *, nc):
    a1 = a1_ref[...]
    a2 = a2_ref[...]

    def body(c, h):
        h_ref[:, pl.ds(c, 1), :] = h
        s = s_ref[:, pl.ds(c, 1), :]
        return a1 * h + a2 * pltpu.roll(h, SSM_STATE, 2) + s

    hf_ref[...] = lax.fori_loop(0, nc, body, h0_ref[...])


def _ssm_y_kernel(u_ref, h_ref, m_ref, q_ref, y_ref):
    y_ref[...] = (jnp.dot(u_ref[...], m_ref[...], preferred_element_type=F32, precision=HIGHEST)
                  + jnp.dot(h_ref[...], q_ref[...], preferred_element_type=F32, precision=HIGHEST))


def s5_mix(u_ssm, h0_re, h0_im, mats):
    m, p, q, a1, a2 = mats
    b, l, w = u_ssm.shape
    g = w // SSM_GROUP
    n2 = 2 * SSM_STATE
    nc = l // SSM_CHUNK
    rows = b * nc
    kw = SSM_CHUNK * SSM_GROUP
    ug = u_ssm.reshape(rows, SSM_CHUNK, g, SSM_GROUP).transpose(2, 0, 1, 3).reshape(g, rows, kw)
    grp = lambda *shape: pl.BlockSpec((None,) + shape, lambda gi: (gi, 0, 0))
    s = pl.pallas_call(
        _ssm_s_kernel, grid=(g,),
        in_specs=[grp(rows, kw), grp(kw, n2)], out_specs=grp(rows, n2),
        out_shape=jax.ShapeDtypeStruct((g, rows, n2), F32),
        compiler_params=_params(("parallel",)), name="ssm_chunk_state",
    )(ug, p)
    h0 = jnp.concatenate([h0_re, h0_im], axis=-1).transpose(1, 0, 2).reshape(g * b, 1, n2)
    hs, hf = pl.pallas_call(
        functools.partial(_ssm_scan_kernel, nc=nc),
        out_shape=[jax.ShapeDtypeStruct((g * b, nc, n2), F32), jax.ShapeDtypeStruct((g * b, 1, n2), F32)],
        compiler_params=pltpu.CompilerParams(vmem_limit_bytes=VMEM_LIMIT), name="ssm_scan",
    )(s.reshape(g * b, nc, n2), jnp.repeat(a1, b, axis=0), jnp.repeat(a2, b, axis=0), h0)
    y = pl.pallas_call(
        _ssm_y_kernel, grid=(g,),
        in_specs=[grp(rows, kw), grp(rows, n2), grp(kw, kw), grp(n2, kw)], out_specs=grp(rows, kw),
        out_shape=jax.ShapeDtypeStruct((g, rows, kw), F32),
        compiler_params=_params(("parallel",)), name="ssm_chunk_out",
    )(ug, hs.reshape(g, rows, n2), m, q)
    y = y.reshape(g, rows, SSM_CHUNK, SSM_GROUP).transpose(1, 2, 0, 3).reshape(b, l, w)
    hf = hf.reshape(g, b, n2).transpose(1, 0, 2)
    return y, hf[..., :SSM_STATE], hf[..., SSM_STATE:]


def _whole(arr):
    return pl.BlockSpec(arr.shape, lambda i: (0,) * arr.ndim, pipeline_mode=pl.Buffered(1))


def _merge_kernel(pm_ref, oa_ref, ys_ref, g0_ref, g1_ref, g2_ref,
                  wglu_ref, bglu_ref, wpo_ref, wao_ref, wso_ref, o_ref):
    gl = jax.nn.gelu(ys_ref[...])
    z = gl * jax.nn.sigmoid(jnp.dot(gl.astype(BF16), wglu_ref[...], preferred_element_type=F32) + bglu_ref[...])
    merged = g0_ref[...] * jnp.dot(pm_ref[...], wpo_ref[...], preferred_element_type=F32)
    merged = merged + g1_ref[...] * jnp.dot(oa_ref[...], wao_ref[...], preferred_element_type=F32)
    merged = merged + g2_ref[...] * jnp.dot(z.astype(BF16), wso_ref[...], preferred_element_type=F32)
    o_ref[...] = merged.astype(BF16)


def merge_branches(pm, oa, ys, gates, wts, tm):
    t = pm.shape[0]
    d = wts["w_o"].shape[0]

    def rows(width, col=0):
        return pl.BlockSpec((tm, width), lambda i: (i, col))

    consts = [wts["w_glu"], wts["b_glu"], wts["w_pool_out"], wts["w_attn_out"], wts["w_ssm_out"]]
    return pl.pallas_call(
        _merge_kernel,
        grid=(t // tm,),
        in_specs=[rows(pm.shape[1]), rows(oa.shape[1]), rows(ys.shape[1]), rows(d, 0), rows(d, 1), rows(d, 2)]
                 + [_whole(a) for a in consts],
        out_specs=rows(d),
        out_shape=jax.ShapeDtypeStruct((t, d), BF16),
        compiler_params=_params(("parallel",)),
        name="merge_branches",
    )(pm, oa, ys, gates, gates, gates, *consts)


def _out_proj_kernel(m_ref, x_ref, gate1_ref, wo_ref, n2g_ref, sh2_ref, sc2_ref, wr_ref, br_ref,
                     xo_ref, h2_ref, lg_ref):
    x = x_ref[...] + gate1_ref[...] * jnp.dot(m_ref[...], wo_ref[...], preferred_element_type=F32)
    xo_ref[...] = x
    ms = jnp.mean(x * x, axis=-1, keepdims=True)
    h2 = x * lax.rsqrt(ms + RMS_EPS) * n2g_ref[...] * (1.0 + sc2_ref[...]) + sh2_ref[...]
    h2_hi = h2.astype(BF16)
    h2_ref[...] = h2_hi
    h2_lo = (h2 - h2_hi.astype(F32)).astype(BF16)
    w_hi, w_lo = wr_ref[0], wr_ref[1]
    lg_ref[...] = (jnp.dot(h2_hi, w_hi, preferred_element_type=F32) + jnp.dot(h2_hi, w_lo, preferred_element_type=F32)
                   + jnp.dot(h2_lo, w_hi, preferred_element_type=F32) + br_ref[...])


def out_proj(merged, x, gate1, shift2, scale2, wts, rows_per_mod, tm):
    t, d = x.shape
    g1op, g1_spec = _mod_operand(gate1, tm, rows_per_mod)
    sh2, sh2_spec = _mod_operand(shift2, tm, rows_per_mod)
    sc2, sc2_spec = _mod_operand(scale2, tm, rows_per_mod)
    rows = lambda width: pl.BlockSpec((tm, width), lambda i: (i, 0))
    return pl.pallas_call(
        _out_proj_kernel,
        grid=(t // tm,),
        in_specs=[rows(d), rows(d), g1_spec, _whole(wts["w_o"]), _whole(wts["norm2_g"]), sh2_spec, sc2_spec,
                  _whole(wts["w_router"]), _whole(wts["b_router"])],
        out_specs=[rows(d), rows(d), rows(ROUTER_PAD)],
        out_shape=[jax.ShapeDtypeStruct((t, d), F32), jax.ShapeDtypeStruct((t, d), BF16),
                   jax.ShapeDtypeStruct((t, ROUTER_PAD), F32)],
        compiler_params=_params(("parallel",)),
        name="out_proj",
    )(merged, x, g1op, wts["w_o"], wts["norm2_g"], sh2, sc2, wts["w_router"], wts["b_router"])


def _moe_kernel(be_ref, nv_ref, x_ref, wg_ref, wu_ref, bg_ref, bu_ref, wd_ref, bd_ref, o_ref, acc_ref, *, nf):
    i, f = pl.program_id(0), pl.program_id(1)

    @pl.when(i < nv_ref[0])
    def _():
        wg, wu, wd = wg_ref[...], wu_ref[...], wd_ref[...]
        rows = x_ref.shape[0] // MOE_ROW_GROUPS
        for c in range(MOE_ROW_GROUPS):
            rs = pl.ds(c * rows, rows)
            x = x_ref[rs, :]
            hg = jnp.dot(x, wg, preferred_element_type=F32) + bg_ref[...]
            hu = jnp.dot(x, wu, preferred_element_type=F32) + bu_ref[...]
            gate = jnp.minimum(hg, SWIGLU_LIMIT)
            up = jnp.clip(hu, -SWIGLU_LIMIT, SWIGLU_LIMIT)
            act = gate * jax.nn.sigmoid(SWIGLU_ALPHA * gate) * (up + 1.0)
            part = jnp.dot(act.astype(BF16), wd, preferred_element_type=F32)
            acc = jnp.where(f == 0, part, acc_ref[rs, :] + part)
            acc_ref[rs, :] = acc
            o_ref[rs, :] = (acc + bd_ref[...]).astype(o_ref.dtype)


def moe_experts(x_sorted, blk_expert, n_valid, w_up, b_up, w_down, b_down, tm, tf):
    s, d = x_sorted.shape
    e, _, f2 = w_up.shape
    ff = f2 // 2
    nf = ff // tf

    def blk(i, nv):
        return jnp.minimum(i, nv[0] - 1)

    grid_spec = pltpu.PrefetchScalarGridSpec(
        num_scalar_prefetch=2,
        grid=(s // tm, nf),
        in_specs=[pl.BlockSpec((tm, d), lambda i, f, be, nv: (blk(i, nv), 0)),
                  pl.BlockSpec((None, d, tf), lambda i, f, be, nv: (be[blk(i, nv)], 0, f)),
                  pl.BlockSpec((None, d, tf), lambda i, f, be, nv: (be[blk(i, nv)], 0, nf + f)),
                  pl.BlockSpec((None, 1, tf), lambda i, f, be, nv: (be[blk(i, nv)], 0, f)),
                  pl.BlockSpec((None, 1, tf), lambda i, f, be, nv: (be[blk(i, nv)], 0, nf + f)),
                  pl.BlockSpec((None, tf, d), lambda i, f, be, nv: (be[blk(i, nv)], f, 0)),
                  pl.BlockSpec((None, 1, d), lambda i, f, be, nv: (be[blk(i, nv)], 0, 0))],
        out_specs=pl.BlockSpec((tm, d), lambda i, f, be, nv: (blk(i, nv), 0)),
        scratch_shapes=[pltpu.VMEM((tm, d), F32)],
    )
    b_up3 = b_up.reshape(e, 1, f2)
    return pl.pallas_call(
        functools.partial(_moe_kernel, nf=nf),
        grid_spec=grid_spec,
        out_shape=jax.ShapeDtypeStruct((s, d), BF16),
        compiler_params=_params(("arbitrary", "arbitrary")),
        name="moe_experts",
    )(blk_expert, n_valid, x_sorted, w_up, w_up, b_up3, b_up3, w_down, b_down.reshape(e, 1, d))


def _combine_kernel(x_ref, *refs):
    y_refs, (gt_ref, gate2_ref, o_ref) = refs[:TOP_K], refs[TOP_K:]
    gt = gt_ref[...]
    acc = gt[:, 0:1] * y_refs[0][...].astype(F32)
    for k in range(1, TOP_K):
        acc = acc + gt[:, k:k + 1] * y_refs[k][...].astype(F32)
    o_ref[...] = x_ref[...] + gate2_ref[...] * acc


def moe_combine(x, y_gathered, gates, gate2, rows_per_mod, tm):
    t, d = x.shape
    g2op, g2_spec = _mod_operand(gate2, tm, rows_per_mod)
    row_spec = pl.BlockSpec((tm, d), lambda i: (i, 0))
    return pl.pallas_call(
        _combine_kernel,
        grid=(t // tm,),
        in_specs=[row_spec] + [row_spec] * TOP_K + [pl.BlockSpec((tm, TOP_K), lambda i: (i, 0)), g2_spec],
        out_specs=row_spec,
        out_shape=jax.ShapeDtypeStruct((t, d), F32),
        compiler_params=_params(("parallel",)),
        name="moe_combine",
    )(x, *y_gathered, gates, g2op)


def _route(logits, n_experts, tm):
    t = logits.shape[0]
    top_v, top_i = lax.top_k(logits, TOP_K)
    gates = jax.nn.softmax(top_v, axis=-1)
    tk = t * TOP_K
    flat_e = top_i.reshape(tk)
    onehot = (flat_e[:, None] == jnp.arange(n_experts, dtype=flat_e.dtype)[None, :]).astype(jnp.int32)
    csum = jnp.cumsum(onehot, axis=0)
    rank = jnp.take_along_axis(csum, flat_e[:, None], axis=1)[:, 0] - 1
    counts = csum[-1]
    padded = (counts + tm - 1) // tm * tm
    pad_end = jnp.cumsum(padded)
    pad_start = pad_end - padded
    dest = pad_start[flat_e] + rank
    n_blocks = -(-tk // tm) + n_experts
    blk_start = jnp.arange(n_blocks, dtype=jnp.int32) * tm
    blk_expert = jnp.minimum(jnp.sum(pad_end[None, :] <= blk_start[:, None], axis=1), n_experts - 1).astype(jnp.int32)
    n_valid = (pad_end[-1] // tm).astype(jnp.int32).reshape(1)
    order = jnp.argsort(flat_e, stable=True).astype(jnp.int32)
    start = jnp.cumsum(counts) - counts
    slot = jnp.arange(n_blocks * tm, dtype=jnp.int32)
    slot_e = jnp.repeat(blk_expert, tm)
    slot_rank = slot - pad_start[slot_e]
    src = jnp.clip(start[slot_e] + slot_rank, 0, tk - 1)
    slot_tok = jnp.where(slot_rank < counts[slot_e], order[src] // TOP_K, slot % t).astype(jnp.int32)
    return slot_tok, dest.reshape(t, TOP_K), gates, blk_expert, n_valid


def _pick_tile(n, pref):
    return pref if n % pref == 0 else n


def _mixers(x, mods, pos0, pool_prev16, h0_re, h0_im, kv_cache, expert_w, layer, wts, ssm_mats, lam_init):
    b, l, d = x.shape
    t = b * l
    shift1, scale1, gate1, shift2, scale2, _ = mods
    xt = x.reshape(t, d)
    tm = _pick_tile(t, 512)
    tm_in = _pick_tile(t, 1024)
    u = norm_matmul(xt, wts["norm1_g"], shift1, scale1, wts["w_in"], wts["b_in"],
                    rows_per_mod=l, act="none", out_dtype=F32, tm=tm_in)
    gates = norm_matmul(xt, wts["norm1_g"], shift1, scale1, wts["w_gate"], wts["b_gate"],
                        rows_per_mod=l, act="sigmoid", out_dtype=BF16, tm=tm_in)
    u3 = u.reshape(b, l, u.shape[1])
    pos = pos0 + jnp.arange(l, dtype=jnp.int32)
    tl = _pick_tile(l, 512)
    q2, kb, vb, k_rows, v_rows = qk_prep(u3, wts["q_norm_g"], wts["k_norm_g"], pos, tl)
    expert_h = None
    if kv_cache is None:
        oa, *expert_h = prompt_diff_attn(q2, kb, vb, wts["q_norm_g"], wts["k_norm_g"], wts["diff_lambda"],
                                         wts["subln_g"], expert_w[0], expert_w[1], layer, lam_init, tl)
    else:
        oa = cached_diff_attn(q2, kb, vb, kv_cache[0], kv_cache[1], layer, wts["diff_lambda"], wts["subln_g"],
                              lam_init)
    aw = ATT_HEADS * V_DIM
    pw = wts["pool_scale"].shape[0]
    pm = pool_mix(u3, (3 * aw) // pw, pool_prev16, wts["w_pool_mix"], wts["pool_scale"], pos0, tl)
    pool_new = u3[:, l - POOL_HIST:, 3 * aw:3 * aw + pw]
    ys, ssm_re, ssm_im = s5_mix(u3[:, :, 3 * aw + pw:], h0_re, h0_im, ssm_mats)
    merged = merge_branches(pm.reshape(t, pw), oa.reshape(t, aw), ys.reshape(t, -1), gates, wts, tm)
    x_new, h2, logits = out_proj(merged, xt, gate1, shift2, scale2, wts, l, tm)
    outs = (k_rows.reshape(b, l, ATT_HEADS, 2 * QK_DIM), v_rows.reshape(b, l, ATT_HEADS, V_DIM),
            pool_new, ssm_re, ssm_im)
    return x_new, h2, logits, outs, expert_h


def kernel(x_prompt, x_sample, c_prompt, c_sample, cache_k, cache_v, cache_pool, state_ssm_re, state_ssm_im, w_ada, b_ada, norm1_g, w_in, w_gate, b_gate, w_pool_mix, pool_scale, w_pool_out, q_norm_g, k_norm_g, diff_lambda, subln_g, w_attn_out, ssm_lambda_re, ssm_lambda_im, ssm_log_dt, ssm_b_re, ssm_b_im, ssm_c_re, ssm_c_im, ssm_d, w_glu, b_glu, w_ssm_out, w_o, norm2_g, w_router, b_router, w_up, b_up, w_down, b_down):
    nb, seq, d = x_prompt.shape
    db, dseq, _ = x_sample.shape
    depth = w_ada.shape[0]
    past = cache_k.shape[2]
    n_exp = w_router.shape[-1]
    aw = ATT_HEADS * V_DIM
    pw = pool_scale.shape[-1]
    tp, ts = nb * seq, db * dseq

    n_c = nb + db
    c_rows = -(-n_c // 8) * 8
    c_pad = jnp.zeros((c_rows, d), F32).at[:n_c].set(jnp.concatenate([c_prompt, c_sample], axis=0))
    mod_all = ada_mod(c_pad, w_ada, b_ada)

    cache_k4 = cache_k.reshape(depth, db, past, aw)
    cache_v4 = cache_v.reshape(depth, db, past, aw)
    pool_zero = jnp.zeros((nb, HIST_ROWS, pw), F32)
    ssm_zero = jnp.zeros((nb,) + state_ssm_re.shape[2:], F32)
    pad_hist = HIST_ROWS - POOL_HIST
    moe_tm, moe_tf = 1024, 512

    xp, xs = x_prompt, x_sample
    outs_p, outs_s = [], []
    for l in range(depth):
        lam_init = 0.8 - 0.6 * math.exp(-0.3 * l)
        wi = w_in[l]
        w_in_r = jnp.concatenate([wi[:, pw:pw + 3 * aw], wi[:, :pw], wi[:, pw + 3 * aw:]], axis=1).astype(BF16)
        wr = jnp.zeros((d, ROUTER_PAD), F32).at[:, :n_exp].set(w_router[l])
        wr_hi = wr.astype(BF16)
        wr = jnp.stack([wr_hi, (wr - wr_hi.astype(F32)).astype(BF16)])
        br = jnp.zeros((1, ROUTER_PAD), F32).at[0, :n_exp].set(b_router[l])
        wts = {
            "norm1_g": norm1_g[l], "w_in": w_in_r, "b_in": jnp.zeros((w_in_r.shape[1],), F32),
            "w_gate": w_gate[l].astype(BF16), "b_gate": b_gate[l],
            "w_pool_mix": w_pool_mix[l], "pool_scale": pool_scale[l],
            "q_norm_g": q_norm_g[l], "k_norm_g": k_norm_g[l], "diff_lambda": diff_lambda[l], "subln_g": subln_g[l],
            "w_glu": w_glu[l].astype(BF16), "b_glu": b_glu[l].reshape(1, -1),
            "w_pool_out": w_pool_out[l].astype(BF16), "w_attn_out": w_attn_out[l].astype(BF16),
            "w_ssm_out": w_ssm_out[l].astype(BF16), "w_o": w_o[l].astype(BF16),
            "norm2_g": norm2_g[l].reshape(1, d), "w_router": wr, "b_router": br,
        }
        ssm_mats = _ssm_matrices(ssm_lambda_re[l], ssm_lambda_im[l], ssm_log_dt[l], ssm_b_re[l], ssm_b_im[l],
                                 ssm_c_re[l], ssm_c_im[l], ssm_d[l])
        mods = jnp.split(mod_all[l], 6, axis=-1)
        mods_p = [m[:nb] for m in mods]
        mods_s = [m[nb:n_c] for m in mods]
        prev_s = jnp.pad(cache_pool[l], ((0, 0), (pad_hist, 0), (0, 0)))

        xp_mid, h2_p, lg_p, o_p, (w_up_h, w_down_h) = _mixers(
            xp, mods_p, 0, pool_zero, ssm_zero, ssm_zero, None, (w_up, w_down), l, wts, ssm_mats, lam_init)
        xs_mid, h2_s, lg_s, o_s, _ = _mixers(xs, mods_s, past, prev_s, state_ssm_re[l], state_ssm_im[l],
                                             (cache_k4, cache_v4), None, l, wts, ssm_mats, lam_init)
        outs_p.append(o_p)
        outs_s.append(o_s)

        h2_all = jnp.concatenate([h2_p, h2_s], axis=0)
        logits = jnp.concatenate([lg_p[:, :n_exp], lg_s[:, :n_exp]], axis=0)
        slot_tok, dest, gates, blk_expert, n_valid = _route(logits, n_exp, moe_tm)
        y_sorted = moe_experts(h2_all[slot_tok], blk_expert, n_valid, w_up_h, b_up[l], w_down_h, b_down[l], moe_tm,
                               moe_tf)
        yg_p = [y_sorted[dest[:tp, k]] for k in range(TOP_K)]
        yg_s = [y_sorted[dest[tp:, k]] for k in range(TOP_K)]
        xp = moe_combine(xp_mid, yg_p, gates[:tp], mods_p[5], seq, _pick_tile(tp, 512)).reshape(nb, seq, d)
        xs = moe_combine(xs_mid, yg_s, gates[tp:], mods_s[5], dseq, _pick_tile(ts, 512)).reshape(db, dseq, d)

    stack = lambda outs, i: jnp.stack([o[i] for o in outs])
    return (xp, xs,
            stack(outs_p, 0), stack(outs_p, 1), stack(outs_p, 2), stack(outs_p, 3), stack(outs_p, 4),
            stack(outs_s, 0), stack(outs_s, 1), stack(outs_s, 2), stack(outs_s, 3), stack(outs_s, 4))
```

```python
import functools
import math

import jax
import jax.numpy as jnp
from jax import lax
from jax.experimental import pallas as pl
from jax.experimental.pallas import tpu as pltpu

F32 = jnp.float32
BF16 = jnp.bfloat16
HIGHEST = lax.Precision.HIGHEST

RMS_EPS = 1e-6
NEG_INF = -1e30
CHUNK = 64
POOL_WINDOWS = (2, 4, 8, 16)
POOL_HIST = 15
HIST_ROWS = 16
ATT_HEADS = 8
QK_DIM = 64
V_DIM = 128
ROT_DIM = 16
ROPE_THETA = 500000.0
SSM_GROUP = 16
SSM_STATE = 64
SSM_CHUNK = 16
TOP_K = 4
MOE_ROW_GROUPS = 2
SWIGLU_LIMIT = 7.0
SWIGLU_ALPHA = 1.702
LANES = 128
LOG2E = math.log2(math.e)
FLASH_SUBTILE = 256
MAX_SHIFT_SPAN = 100.0
ROUTER_PAD = 128
VMEM_LIMIT = 52 * 1024 * 1024


def _params(sem, vmem=VMEM_LIMIT):
    return pltpu.CompilerParams(dimension_semantics=sem, vmem_limit_bytes=vmem)


def _mod_operand(arr, tm, rows_per_mod):
    nb, d = arr.shape
    if rows_per_mod % tm == 0:
        return arr.reshape(nb, 1, d), pl.BlockSpec((None, 1, d), lambda i, *_: ((i * tm) // rows_per_mod, 0, 0))
    return jnp.repeat(arr, rows_per_mod, axis=0), pl.BlockSpec((tm, d), lambda i, *_: (i, 0))


def _ada_kernel(c_ref, w_ref, b_ref, o_ref):
    c = c_ref[...]
    s = c * jax.nn.sigmoid(c)
    o_ref[...] = jnp.dot(s.astype(BF16), w_ref[...].astype(BF16), preferred_element_type=F32) + b_ref[...]


def ada_mod(c_pad, w_ada, b_ada):
    depth, d, n = w_ada.shape
    rows = c_pad.shape[0]
    tn = 1024
    return pl.pallas_call(
        _ada_kernel,
        grid=(depth, n // tn),
        in_specs=[pl.BlockSpec((rows, d), lambda l, j: (0, 0)),
                  pl.BlockSpec((None, d, tn), lambda l, j: (l, 0, j)),
                  pl.BlockSpec((None, 1, tn), lambda l, j: (l, 0, j))],
        out_specs=pl.BlockSpec((None, rows, tn), lambda l, j: (l, 0, j)),
        out_shape=jax.ShapeDtypeStruct((depth, rows, n), F32),
        compiler_params=_params(("parallel", "parallel")),
        name="ada_mod",
    )(c_pad, w_ada, b_ada.reshape(depth, 1, n))


def _norm_mm_kernel(x_ref, g_ref, sh_ref, sc_ref, w_ref, b_ref, u_ref, gt_ref, h_scr, *, n_lin):
    j = pl.program_id(1)

    @pl.when(j == 0)
    def _():
        x = x_ref[...]
        ms = jnp.mean(x * x, axis=-1, keepdims=True)
        y = x * lax.rsqrt(ms + RMS_EPS) * g_ref[...]
        h_scr[...] = (y * (1.0 + sc_ref[...]) + sh_ref[...]).astype(BF16)

    r = jnp.dot(h_scr[...], w_ref[...], preferred_element_type=F32) + b_ref[...]

    @pl.when(j < n_lin)
    def _():
        u_ref[...] = r

    @pl.when(j >= n_lin)
    def _():
        gt_ref[...] = jax.nn.sigmoid(r).astype(gt_ref.dtype)


def norm_matmul(x, gain, shift, scale, w, bias, n_lin_cols, *, rows_per_mod, tm):
    t, d = x.shape
    n = w.shape[1]
    tn = 512
    n_lin = n_lin_cols // tn
    assert n_lin * tn == n_lin_cols and n % tn == 0 and 0 < n_lin < n // tn
    sh, sh_spec = _mod_operand(shift, tm, rows_per_mod)
    sc, sc_spec = _mod_operand(scale, tm, rows_per_mod)
    return pl.pallas_call(
        functools.partial(_norm_mm_kernel, n_lin=n_lin),
        grid=(t // tm, n // tn),
        in_specs=[pl.BlockSpec((tm, d), lambda i, j: (i, 0)),
                  pl.BlockSpec((1, d), lambda i, j: (0, 0)),
                  sh_spec, sc_spec,
                  pl.BlockSpec((d, tn), lambda i, j: (0, j)),
                  pl.BlockSpec((1, tn), lambda i, j: (0, j))],
        out_specs=[pl.BlockSpec((tm, tn), lambda i, j: (i, jnp.minimum(j, n_lin - 1))),
                   pl.BlockSpec((tm, tn), lambda i, j: (i, jnp.maximum(j - n_lin, 0)))],
        out_shape=[jax.ShapeDtypeStruct((t, n_lin_cols), F32), jax.ShapeDtypeStruct((t, n - n_lin_cols), BF16)],
        scratch_shapes=[pltpu.VMEM((tm, d), BF16)],
        compiler_params=_params(("parallel", "arbitrary")),
        name="norm_matmul",
    )(x, gain.reshape(1, d), sh, sc, w, bias.reshape(1, n))


def _rope_tables(pos):
    half = ROT_DIM // 2
    inv_freq = jnp.exp(-math.log(ROPE_THETA) * jnp.arange(half, dtype=F32) * (2.0 / ROT_DIM))
    ang = pos.astype(F32)[:, None] * inv_freq[None, :]
    cos, sin = jnp.cos(ang), jnp.sin(ang)
    lane = jnp.arange(LANES) % QK_DIM
    idx = lane % half
    c_t = jnp.where(lane[None, :] < ROT_DIM, cos[:, idx], 1.0)
    s1_t = jnp.where((lane[None, :] >= half) & (lane[None, :] < ROT_DIM), sin[:, idx], 0.0)
    s2_t = jnp.where(lane[None, :] < half, -sin[:, idx], 0.0)
    return c_t.astype(F32), s1_t.astype(F32), s2_t.astype(F32)


def _qk_prep_kernel(q_ref, k_ref, v_ref, qg_ref, kg_ref, c_ref, s1_ref, s2_ref, gm_ref,
                    q2_ref, kb_ref, vb_ref, krow_ref, vrow_ref):
    c_t, s1_t, s2_t = c_ref[...], s1_ref[...], s2_ref[...]
    gm = gm_ref[...]
    lane = lax.broadcasted_iota(jnp.int32, (1, LANES), 1)
    first_map = lane < QK_DIM

    def norm_rot(x, g):
        ms = jnp.dot(x * x, gm, preferred_element_type=F32, precision=HIGHEST)
        y = x * lax.rsqrt(ms + RMS_EPS) * g
        return y * c_t + pltpu.roll(y, ROT_DIM // 2, 1) * s1_t + pltpu.roll(y, LANES - ROT_DIM // 2, 1) * s2_t

    for h in range(ATT_HEADS):
        sl = slice(h * LANES, (h + 1) * LANES)
        qr = norm_rot(q_ref[:, sl], qg_ref[...]) * (QK_DIM ** -0.5 * LOG2E)
        q2_ref[h, 0] = jnp.where(first_map, qr, 0.0).astype(BF16)
        q2_ref[h, 1] = jnp.where(first_map, 0.0, qr).astype(BF16)
        kr = norm_rot(k_ref[:, sl], kg_ref[...])
        krow_ref[:, sl] = kr
        kb_ref[h] = kr.astype(BF16)
        v = v_ref[:, sl]
        vb_ref[h] = v.astype(BF16)
    vrow_ref[...] = v_ref[...]


def qk_prep(u3, q_norm_g, k_norm_g, pos, tm):
    b, l, _ = u3.shape
    aw = ATT_HEADS * V_DIM
    c_t, s1_t, s2_t = _rope_tables(pos)
    qg = jnp.tile(q_norm_g, 2).reshape(1, LANES)
    kg = jnp.tile(k_norm_g, 2).reshape(1, LANES)
    grp = jnp.arange(LANES) // QK_DIM
    gm = jnp.where(grp[:, None] == grp[None, :], 1.0 / QK_DIM, 0.0).astype(F32)
    tab = pl.BlockSpec((tm, LANES), lambda bi, i: (i, 0))
    vec = pl.BlockSpec((1, LANES), lambda bi, i: (0, 0))
    return pl.pallas_call(
        _qk_prep_kernel,
        grid=(b, l // tm),
        in_specs=[pl.BlockSpec((None, tm, aw), lambda bi, i: (bi, i, 0)),
                  pl.BlockSpec((None, tm, aw), lambda bi, i: (bi, i, 1)),
                  pl.BlockSpec((None, tm, aw), lambda bi, i: (bi, i, 2)),
                  vec, vec, tab, tab, tab,
                  pl.BlockSpec((LANES, LANES), lambda bi, i: (0, 0))],
        out_specs=[pl.BlockSpec((None, ATT_HEADS, 2, tm, LANES), lambda bi, i: (bi, 0, 0, i, 0)),
                   pl.BlockSpec((None, ATT_HEADS, tm, LANES), lambda bi, i: (bi, 0, i, 0)),
                   pl.BlockSpec((None, ATT_HEADS, tm, LANES), lambda bi, i: (bi, 0, i, 0)),
                   pl.BlockSpec((None, tm, aw), lambda bi, i: (bi, i, 0)),
                   pl.BlockSpec((None, tm, aw), lambda bi, i: (bi, i, 0))],
        out_shape=[jax.ShapeDtypeStruct((b, ATT_HEADS, 2, l, LANES), BF16),
                   jax.ShapeDtypeStruct((b, ATT_HEADS, l, LANES), BF16),
                   jax.ShapeDtypeStruct((b, ATT_HEADS, l, LANES), BF16),
                   jax.ShapeDtypeStruct((b, l, aw), F32),
                   jax.ShapeDtypeStruct((b, l, aw), F32)],
        compiler_params=_params(("parallel", "parallel")),
        name="qk_prep",
    )(u3, u3, u3, qg, kg, c_t, s1_t, s2_t, gm)


def _diff_lambda(dl_ref, lam_init):
    dl = dl_ref[...]
    a = jnp.sum(dl[0:1] * dl[1:2], axis=1, keepdims=True)
    b = jnp.sum(dl[2:3] * dl[3:4], axis=1, keepdims=True)
    return jnp.exp(a) - jnp.exp(b) + lam_init


def _diff_finish(o0, o1, dl_ref, sg_ref, lam_init):
    od = o0 - _diff_lambda(dl_ref, lam_init) * o1
    ms = jnp.mean(od * od, axis=-1, keepdims=True)
    return od * lax.rsqrt(ms + RMS_EPS) * sg_ref[...] * (1.0 - lam_init)


def _softmax_step(s, v, m, l, acc):
    m_new = jnp.maximum(m, jnp.max(s, axis=1, keepdims=True))
    alpha = jnp.exp2(m - m_new)
    p = jnp.exp2(s - m_new)
    l_new = alpha * l + jnp.sum(p, axis=1, keepdims=True)
    acc_new = alpha * acc + jnp.dot(p.astype(BF16), v, preferred_element_type=F32)
    return m_new, l_new, acc_new


_NT = (((1,), (1,)), ((), ()))


def _cast_rows(wu_ref, wd_ref, wuo_ref, wdo_ref):
    wuo_ref[...] = wu_ref[...].astype(BF16)
    wdo_ref[...] = wd_ref[...].astype(BF16)


def _flash_kernel(q_ref, k_ref, v_ref, dl_ref, sg_ref, wu_ref, wd_ref, o_ref, wuo_ref, wdo_ref, *, tq, tk, lam_init):
    _cast_rows(wu_ref, wd_ref, wuo_ref, wdo_ref)
    qi = pl.program_id(2)
    qs = (q_ref[0], q_ref[1])
    n_full = (qi * tq) // tk

    def step(off, carries, visible):
        k = k_ref[pl.ds(off, tk), :]
        v = v_ref[pl.ds(off, tk), :]
        scores = [lax.dot_general(q, k, _NT, preferred_element_type=F32) for q in qs]
        out = []
        for s, carry in zip(scores, carries):
            if visible is not None:
                s = jnp.where(visible, s, NEG_INF)
            out.append(_softmax_step(s, v, *carry))
        return tuple(out)

    def body(kb, carries):
        return step(pl.multiple_of(kb * tk, tk), carries, None)

    init = (jnp.full((tq, 1), NEG_INF, F32), jnp.zeros((tq, 1), F32), jnp.zeros((tq, V_DIM), F32))
    carries = lax.fori_loop(0, n_full, body, (init, init))

    off = pl.multiple_of(n_full * tk, tk)
    row = qi * tq + lax.broadcasted_iota(jnp.int32, (tq, tk), 0)
    col = off + lax.broadcasted_iota(jnp.int32, (tq, tk), 1)
    (_, l0, acc0), (_, l1, acc1) = step(off, carries, (col // CHUNK) <= (row // CHUNK))
    o_ref[...] = _diff_finish(acc0 / l0, acc1 / l1, dl_ref, sg_ref, lam_init).astype(o_ref.dtype)


def _shifted_flash_kernel(c_ref, q_ref, k_ref, v_ref, dl_ref, sg_ref, wu_ref, wd_ref, o_ref, wuo_ref, wdo_ref,
                          acc_ref, l_ref, *, tq, tk, ks, lam_init):
    qi = pl.program_id(2)
    n_full = (qi * tq) // tk
    shift = c_ref[...]
    acc_ref[...] = jnp.zeros_like(acc_ref)
    l_ref[...] = jnp.zeros_like(l_ref)

    q = q_ref[...].reshape(2 * tq, LANES)

    def sub_tile(start, masked):
        k = k_ref[pl.ds(start, ks), :]
        v = v_ref[pl.ds(start, ks), :]
        s = lax.dot_general(q, k, _NT, preferred_element_type=F32)
        p = jnp.exp2(s - shift)
        if masked:
            row = qi * tq + (lax.broadcasted_iota(jnp.int32, (2 * tq, ks), 0) & (tq - 1))
            col = start + lax.broadcasted_iota(jnp.int32, (2 * tq, ks), 1)
            p = jnp.where((col // CHUNK) <= (row // CHUNK), p, 0.0)
        part = p[:, :LANES]
        for t in range(1, ks // LANES):
            part = part + p[:, t * LANES:(t + 1) * LANES]
        l_ref[...] += part
        acc_ref[...] += jnp.dot(p.astype(BF16), v, preferred_element_type=F32)

    def block(kb, masked):
        for j in range(tk // ks):
            sub_tile(pl.multiple_of(kb * tk + j * ks, ks), masked)

    def full_pair(kp, carry):
        block(2 * kp, False)
        block(2 * kp + 1, False)
        return carry

    lax.fori_loop(0, n_full // 2, full_pair, 0)

    @pl.when(n_full % 2 == 1)
    def _():
        block(n_full - 1, False)

    _cast_rows(wu_ref, wd_ref, wuo_ref, wdo_ref)
    block(n_full, True)
    o = acc_ref[...] / jnp.sum(l_ref[...], axis=1, keepdims=True)
    o_ref[...] = _diff_finish(o[:tq], o[tq:], dl_ref, sg_ref, lam_init).astype(o_ref.dtype)


def _prompt_attn_call(kernel_fn, name, lead, scratch, q2, kb, vb, diff_lambda, subln_g, w_up, w_down, layer, tq):
    b, h, _, l, _ = q2.shape
    nq = l // tq
    _, e, d_in, f2 = w_up.shape
    _, _, d_ff, d_out = w_down.shape
    steps = b * h * nq
    ru, rd = e * d_in // steps, e * d_ff // steps
    assert ru * steps == e * d_in and d_in % ru == 0 and rd * steps == e * d_ff and d_ff % rd == 0

    def chunk(rows_per_expert, rows):
        per = rows_per_expert // rows

        def idx(bi, hi, qi):
            c = (bi * h + hi) * nq + qi
            return c // per, c % per
        return idx

    iu, idn = chunk(d_in, ru), chunk(d_ff, rd)
    const = lambda shape: pl.BlockSpec(shape, lambda bi, hi, qi: (0,) * len(shape))
    in_specs = ([const(x.shape) for x in lead]
                + [pl.BlockSpec((None, None, 2, tq, LANES), lambda bi, hi, qi: (bi, hi, 0, qi, 0)),
                   pl.BlockSpec((None, None, l, LANES), lambda bi, hi, qi: (bi, hi, 0, 0)),
                   pl.BlockSpec((None, None, l, LANES), lambda bi, hi, qi: (bi, hi, 0, 0)),
                   const((4, QK_DIM)), const((1, V_DIM)),
                   pl.BlockSpec((None, None, ru, f2), lambda bi, hi, qi: (layer,) + iu(bi, hi, qi) + (0,)),
                   pl.BlockSpec((None, None, rd, d_out), lambda bi, hi, qi: (layer,) + idn(bi, hi, qi) + (0,))])
    out_specs = [pl.BlockSpec((None, tq, V_DIM), lambda bi, hi, qi: (bi, qi, hi)),
                 pl.BlockSpec((None, ru, f2), lambda bi, hi, qi: iu(bi, hi, qi) + (0,)),
                 pl.BlockSpec((None, rd, d_out), lambda bi, hi, qi: idn(bi, hi, qi) + (0,))]
    return pl.pallas_call(
        kernel_fn,
        grid=(b, h, nq),
        in_specs=in_specs,
        out_specs=out_specs,
        out_shape=[jax.ShapeDtypeStruct((b, l, h * V_DIM), BF16),
                   jax.ShapeDtypeStruct((e, d_in, f2), BF16), jax.ShapeDtypeStruct((e, d_ff, d_out), BF16)],
        scratch_shapes=scratch,
        compiler_params=_params(("parallel", "parallel", "arbitrary")),
        name=name,
    )(*lead, q2, kb, vb, diff_lambda, subln_g.reshape(1, V_DIM), w_up, w_down)


def prompt_diff_attn(q2, kb, vb, q_norm_g, k_norm_g, diff_lambda, subln_g, w_up, w_down, layer, lam_init, tq):
    l = q2.shape[3]
    tk = min(2 * tq, l)
    ks = min(FLASH_SUBTILE, tk)
    c = ((QK_DIM ** 0.5 * LOG2E) * jnp.max(jnp.abs(q_norm_g)) * jnp.max(jnp.abs(k_norm_g))).astype(F32)
    args = (q2, kb, vb, diff_lambda, subln_g, w_up, w_down, layer, tq)
    return lax.cond(
        2.0 * c < MAX_SHIFT_SPAN,
        lambda: _prompt_attn_call(
            functools.partial(_shifted_flash_kernel, tq=tq, tk=tk, ks=ks, lam_init=lam_init), "shifted_flash_diff_attn",
            [c.reshape(1, 1)], [pltpu.VMEM((2 * tq, V_DIM), F32), pltpu.VMEM((2 * tq, LANES), F32)], *args),
        lambda: _prompt_attn_call(
            functools.partial(_flash_kernel, tq=tq, tk=tk, lam_init=lam_init), "flash_diff_attn", [], [], *args))


def _cached_attn_kernel(q_ref, ck_ref, cv_ref, k_ref, v_ref, dl_ref, sg_ref, o_ref, *, rows, lam_init):
    q = q_ref[...].reshape(2 * rows, LANES)
    s_c = lax.dot_general(q, ck_ref[...].astype(BF16), _NT, preferred_element_type=F32)
    s_n = lax.dot_general(q, k_ref[...], _NT, preferred_element_type=F32)
    m = jnp.maximum(jnp.max(s_c, axis=1, keepdims=True), jnp.max(s_n, axis=1, keepdims=True))
    p_c = jnp.exp2(s_c - m)
    p_n = jnp.exp2(s_n - m)
    l = jnp.sum(p_c, axis=1, keepdims=True) + jnp.sum(p_n, axis=1, keepdims=True)
    acc = (jnp.dot(p_c.astype(BF16), cv_ref[...].astype(BF16), preferred_element_type=F32)
           + jnp.dot(p_n.astype(BF16), v_ref[...], preferred_element_type=F32))
    o = acc / l
    o_ref[...] = _diff_finish(o[:rows], o[rows:], dl_ref, sg_ref, lam_init).astype(o_ref.dtype)


def cached_diff_attn(q2, kb, vb, cache_k, cache_v, layer, diff_lambda, subln_g, lam_init):
    b, h, _, rows, _ = q2.shape
    past = cache_k.shape[2]
    cache_spec = pl.BlockSpec((None, None, past, LANES), lambda bi, hi: (layer, bi, 0, hi))
    new_spec = pl.BlockSpec((None, None, rows, LANES), lambda bi, hi: (bi, hi, 0, 0))
    return pl.pallas_call(
        functools.partial(_cached_attn_kernel, rows=rows, lam_init=lam_init),
        grid=(b, h),
        in_specs=[pl.BlockSpec((None, None, 2, rows, LANES), lambda bi, hi: (bi, hi, 0, 0, 0)),
                  cache_spec, cache_spec, new_spec, new_spec,
                  pl.BlockSpec((4, QK_DIM), lambda bi, hi: (0, 0)),
                  pl.BlockSpec((1, V_DIM), lambda bi, hi: (0, 0))],
        out_specs=pl.BlockSpec((None, rows, V_DIM), lambda bi, hi: (bi, 0, hi)),
        out_shape=jax.ShapeDtypeStruct((b, rows, h * V_DIM), BF16),
        compiler_params=_params(("parallel", "parallel")),
        name="cached_diff_attn",
    )(q2, cache_k, cache_v, kb, vb, diff_lambda, subln_g.reshape(1, V_DIM))


def _pool_kernel(u_ref, hist_ref, prev_ref, w_ref, sc_ref, o_ref, *, tm, pos0):
    i = pl.program_id(1)
    hist = jnp.where(i == 0, prev_ref[...], hist_ref[...])
    pos = pos0 + i * tm + lax.broadcasted_iota(jnp.int32, (tm, 1), 0)
    gw = LANES
    for g, w in enumerate(POOL_WINDOWS):
        sl = slice(g * gw, (g + 1) * gw)
        ext = jnp.concatenate([hist[:, sl], u_ref[:, sl]], axis=0)
        s = ext
        step = 1
        while step < w:
            s = s + pltpu.roll(s, step, 0)
            step *= 2
        cnt = jnp.minimum(pos + 1, w).astype(F32)
        d = s[HIST_ROWS:] / cnt - ext[HIST_ROWS:]
        y = jnp.dot(d.astype(BF16), w_ref[g], preferred_element_type=F32) * sc_ref[:, sl]
        o_ref[:, sl] = y.astype(o_ref.dtype)


def pool_mix(u3, col_block, prev16, w_mix, pool_scale, pos0, tm):
    b, l, _ = u3.shape
    pw = w_mix.shape[0] * w_mix.shape[1]
    hb = tm // HIST_ROWS
    return pl.pallas_call(
        functools.partial(_pool_kernel, tm=tm, pos0=pos0),
        grid=(b, l // tm),
        in_specs=[pl.BlockSpec((None, tm, pw), lambda bi, i: (bi, i, col_block)),
                  pl.BlockSpec((None, HIST_ROWS, pw), lambda bi, i: (bi, jnp.maximum(i * hb - 1, 0), col_block)),
                  pl.BlockSpec((None, HIST_ROWS, pw), lambda bi, i: (bi, 0, 0)),
                  pl.BlockSpec(w_mix.shape, lambda bi, i: (0, 0, 0)),
                  pl.BlockSpec((1, pw), lambda bi, i: (0, 0))],
        out_specs=pl.BlockSpec((None, tm, pw), lambda bi, i: (bi, i, 0)),
        out_shape=jax.ShapeDtypeStruct((b, l, pw), BF16),
        compiler_params=_params(("parallel", "parallel")),
        name="pool_mix",
    )(u3, u3, prev16, w_mix.astype(BF16), pool_scale.reshape(1, pw))


def _ssm_matrices(lam_re, lam_im, log_dt, b_re, b_im, c_re, c_im, d_skip):
    g, n = lam_re.shape
    cw = SSM_GROUP
    lc = SSM_CHUNK
    dt = jnp.exp(log_dt.astype(F32))[:, None]
    d_idx = jnp.arange(lc + 1, dtype=F32)[None, :, None]
    mag = jnp.exp(lam_re[:, None, :] * dt[:, :, None] * d_idx)
    ang = lam_im[:, None, :] * dt[:, :, None] * d_idx
    pw_re, pw_im = mag * jnp.cos(ang), mag * jnp.sin(ang)
    num_re, num_im = pw_re[:, 1] - 1.0, pw_im[:, 1]
    den = lam_re * lam_re + lam_im * lam_im
    f_re = (num_re * lam_re + num_im * lam_im) / den
    f_im = (num_im * lam_re - num_re * lam_im) / den
    bb_re = f_re[..., None] * b_re - f_im[..., None] * b_im
    bb_im = f_re[..., None] * b_im + f_im[..., None] * b_re
    cp_re = c_re[:, None] * pw_re[:, :, None, :] - c_im[:, None] * pw_im[:, :, None, :]
    cp_im = c_re[:, None] * pw_im[:, :, None, :] + c_im[:, None] * pw_re[:, :, None, :]
    w = (jnp.einsum("gdcn,gne->gdce", cp_re[:, :lc], bb_re, precision=HIGHEST)
         - jnp.einsum("gdcn,gne->gdce", cp_im[:, :lc], bb_im, precision=HIGHEST))
    s_idx = jnp.arange(lc)[:, None]
    t_idx = jnp.arange(lc)[None, :]
    lag = jnp.clip(t_idx - s_idx, 0, lc - 1)
    toe = jnp.where((t_idx >= s_idx)[None, :, :, None, None], w[:, lag], 0.0)
    m = toe.transpose(0, 1, 4, 2, 3).reshape(g, lc * cw, lc * cw)
    dvec = jnp.tile(d_skip.reshape(g, 1, cw), (1, lc, 1)).reshape(g, lc * cw)
    m = m + jnp.eye(lc * cw, dtype=F32)[None] * dvec[:, None, :]
    pr, pi = pw_re[:, lc - 1::-1][:, :lc], pw_im[:, lc - 1::-1][:, :lc]
    p_re = pr[:, :, None, :] * bb_re.transpose(0, 2, 1)[:, None] - pi[:, :, None, :] * bb_im.transpose(0, 2, 1)[:, None]
    p_im = pr[:, :, None, :] * bb_im.transpose(0, 2, 1)[:, None] + pi[:, :, None, :] * bb_re.transpose(0, 2, 1)[:, None]
    p = jnp.concatenate([p_re, p_im], axis=-1).reshape(g, lc * cw, 2 * n)
    q = jnp.concatenate([cp_re[:, 1:].transpose(0, 3, 1, 2), -cp_im[:, 1:].transpose(0, 3, 1, 2)], axis=1)
    q = q.reshape(g, 2 * n, lc * cw)
    a_re, a_im = pw_re[:, lc], pw_im[:, lc]
    a1 = jnp.concatenate([a_re, a_re], axis=-1).reshape(g, 1, 2 * n)
    a2 = jnp.concatenate([-a_im, a_im], axis=-1).reshape(g, 1, 2 * n)
    return m, p, q, a1, a2


def _ssm_s_kernel(u_ref, p_ref, s_ref):
    s_ref[...] = jnp.dot(u_ref[...], p_ref[...], preferred_element_type=F32, precision=HIGHEST)


def _ssm_scan_kernel(s_ref, a1_ref, a2_ref, h0_ref, h_ref, hf_ref, *, nc):
    a1 = a1_ref[...]
    a2 = a2_ref[...]

    def body(c, h):
        h_ref[:, pl.ds(c, 1), :] = h
        s = s_ref[:, pl.ds(c, 1), :]
        return a1 * h + a2 * pltpu.roll(h, SSM_STATE, 2) + s

    hf_ref[...] = lax.fori_loop(0, nc, body, h0_ref[...])


def _ssm_y_kernel(u_ref, h_ref, m_ref, q_ref, y_ref):
    y_ref[...] = (jnp.dot(u_ref[...], m_ref[...], preferred_element_type=F32, precision=HIGHEST)
                  + jnp.dot(h_ref[...], q_ref[...], preferred_element_type=F32, precision=HIGHEST))


def s5_mix(u_ssm, h0_re, h0_im, mats):
    m, p, q, a1, a2 = mats
    b, l, w = u_ssm.shape
    g = w // SSM_GROUP
    n2 = 2 * SSM_STATE
    nc = l // SSM_CHUNK
    rows = b * nc
    kw = SSM_CHUNK * SSM_GROUP
    ug = u_ssm.reshape(rows, SSM_CHUNK, g, SSM_GROUP).transpose(2, 0, 1, 3).reshape(g, rows, kw)
    grp = lambda *shape: pl.BlockSpec((None,) + shape, lambda gi: (gi, 0, 0))
    s = pl.pallas_call(
        _ssm_s_kernel, grid=(g,),
        in_specs=[grp(rows, kw), grp(kw, n2)], out_specs=grp(rows, n2),
        out_shape=jax.ShapeDtypeStruct((g, rows, n2), F32),
        compiler_params=_params(("parallel",)), name="ssm_chunk_state",
    )(ug, p)
    h0 = jnp.concatenate([h0_re, h0_im], axis=-1).transpose(1, 0, 2).reshape(g * b, 1, n2)
    hs, hf = pl.pallas_call(
        functools.partial(_ssm_scan_kernel, nc=nc),
        out_shape=[jax.ShapeDtypeStruct((g * b, nc, n2), F32), jax.ShapeDtypeStruct((g * b, 1, n2), F32)],
        compiler_params=pltpu.CompilerParams(vmem_limit_bytes=VMEM_LIMIT), name="ssm_scan",
    )(s.reshape(g * b, nc, n2), jnp.repeat(a1, b, axis=0), jnp.repeat(a2, b, axis=0), h0)
    y = pl.pallas_call(
        _ssm_y_kernel, grid=(g,),
        in_specs=[grp(rows, kw), grp(rows, n2), grp(kw, kw), grp(n2, kw)], out_specs=grp(rows, kw),
        out_shape=jax.ShapeDtypeStruct((g, rows, kw), F32),
        compiler_params=_params(("parallel",)), name="ssm_chunk_out",
    )(ug, hs.reshape(g, rows, n2), m, q)
    y = y.reshape(g, rows, SSM_CHUNK, SSM_GROUP).transpose(1, 2, 0, 3).reshape(b, l, w)
    hf = hf.reshape(g, b, n2).transpose(1, 0, 2)
    return y, hf[..., :SSM_STATE], hf[..., SSM_STATE:]


def _whole(arr):
    return pl.BlockSpec(arr.shape, lambda i: (0,) * arr.ndim, pipeline_mode=pl.Buffered(1))


def _merge_kernel(pm_ref, oa_ref, ys_ref, g0_ref, g1_ref, g2_ref,
                  wglu_ref, bglu_ref, wpo_ref, wao_ref, wso_ref, o_ref):
    gl = jax.nn.gelu(ys_ref[...])
    z = gl * jax.nn.sigmoid(jnp.dot(gl.astype(BF16), wglu_ref[...], preferred_element_type=F32) + bglu_ref[...])
    merged = g0_ref[...] * jnp.dot(pm_ref[...], wpo_ref[...], preferred_element_type=F32)
    merged = merged + g1_ref[...] * jnp.dot(oa_ref[...], wao_ref[...], preferred_element_type=F32)
    merged = merged + g2_ref[...] * jnp.dot(z.astype(BF16), wso_ref[...], preferred_element_type=F32)
    o_ref[...] = merged.astype(BF16)


def merge_branches(pm, oa, ys, gates, wts, tm):
    t = pm.shape[0]
    d = wts["w_o"].shape[0]

    def rows(width, col=0):
        return pl.BlockSpec((tm, width), lambda i: (i, col))

    consts = [wts["w_glu"], wts["b_glu"], wts["w_pool_out"], wts["w_attn_out"], wts["w_ssm_out"]]
    return pl.pallas_call(
        _merge_kernel,
        grid=(t // tm,),
        in_specs=[rows(pm.shape[1]), rows(oa.shape[1]), rows(ys.shape[1]), rows(d, 0), rows(d, 1), rows(d, 2)]
                 + [_whole(a) for a in consts],
        out_specs=rows(d),
        out_shape=jax.ShapeDtypeStruct((t, d), BF16),
        compiler_params=_params(("parallel",)),
        name="merge_branches",
    )(pm, oa, ys, gates, gates, gates, *consts)


def _out_proj_kernel(m_ref, x_ref, gate1_ref, wo_ref, n2g_ref, sh2_ref, sc2_ref, wr_ref, br_ref,
                     xo_ref, h2_ref, lg_ref):
    x = x_ref[...] + gate1_ref[...] * jnp.dot(m_ref[...], wo_ref[...], preferred_element_type=F32)
    xo_ref[...] = x
    ms = jnp.mean(x * x, axis=-1, keepdims=True)
    h2 = x * lax.rsqrt(ms + RMS_EPS) * n2g_ref[...] * (1.0 + sc2_ref[...]) + sh2_ref[...]
    h2_hi = h2.astype(BF16)
    h2_ref[...] = h2_hi
    h2_lo = (h2 - h2_hi.astype(F32)).astype(BF16)
    w_hi, w_lo = wr_ref[0], wr_ref[1]
    lg_ref[...] = (jnp.dot(h2_hi, w_hi, preferred_element_type=F32) + jnp.dot(h2_hi, w_lo, preferred_element_type=F32)
                   + jnp.dot(h2_lo, w_hi, preferred_element_type=F32) + br_ref[...])


def out_proj(merged, x, gate1, shift2, scale2, wts, rows_per_mod, tm):
    t, d = x.shape
    g1op, g1_spec = _mod_operand(gate1, tm, rows_per_mod)
    sh2, sh2_spec = _mod_operand(shift2, tm, rows_per_mod)
    sc2, sc2_spec = _mod_operand(scale2, tm, rows_per_mod)
    rows = lambda width: pl.BlockSpec((tm, width), lambda i: (i, 0))
    return pl.pallas_call(
        _out_proj_kernel,
        grid=(t // tm,),
        in_specs=[rows(d), rows(d), g1_spec, _whole(wts["w_o"]), _whole(wts["norm2_g"]), sh2_spec, sc2_spec,
                  _whole(wts["w_router"]), _whole(wts["b_router"])],
        out_specs=[rows(d), rows(d), rows(ROUTER_PAD)],
        out_shape=[jax.ShapeDtypeStruct((t, d), F32), jax.ShapeDtypeStruct((t, d), BF16),
                   jax.ShapeDtypeStruct((t, ROUTER_PAD), F32)],
        compiler_params=_params(("parallel",)),
        name="out_proj",
    )(merged, x, g1op, wts["w_o"], wts["norm2_g"], sh2, sc2, wts["w_router"], wts["b_router"])


def _moe_kernel(be_ref, nv_ref, x_ref, wg_ref, wu_ref, bg_ref, bu_ref, wd_ref, bd_ref, o_ref, acc_ref, *, nf):
    i, f = pl.program_id(0), pl.program_id(1)

    @pl.when(i < nv_ref[0])
    def _():
        wg, wu, wd = wg_ref[...], wu_ref[...], wd_ref[...]
        rows = x_ref.shape[0] // MOE_ROW_GROUPS
        for c in range(MOE_ROW_GROUPS):
            rs = pl.ds(c * rows, rows)
            x = x_ref[rs, :]
            hg = jnp.dot(x, wg, preferred_element_type=F32) + bg_ref[...]
            hu = jnp.dot(x, wu, preferred_element_type=F32) + bu_ref[...]
            gate = jnp.minimum(hg, SWIGLU_LIMIT)
            up = jnp.clip(hu, -SWIGLU_LIMIT, SWIGLU_LIMIT)
            act = gate * jax.nn.sigmoid(SWIGLU_ALPHA * gate) * (up + 1.0)
            part = jnp.dot(act.astype(BF16), wd, preferred_element_type=F32)
            acc = jnp.where(f == 0, part, acc_ref[rs, :] + part)
            acc_ref[rs, :] = acc
            o_ref[rs, :] = (acc + bd_ref[...]).astype(o_ref.dtype)


def moe_experts(x_sorted, blk_expert, n_valid, w_up, b_up, w_down, b_down, tm, tf):
    s, d = x_sorted.shape
    e, _, f2 = w_up.shape
    ff = f2 // 2
    nf = ff // tf

    def blk(i, nv):
        return jnp.minimum(i, nv[0] - 1)

    grid_spec = pltpu.PrefetchScalarGridSpec(
        num_scalar_prefetch=2,
        grid=(s // tm, nf),
        in_specs=[pl.BlockSpec((tm, d), lambda i, f, be, nv: (blk(i, nv), 0)),
                  pl.BlockSpec((None, d, tf), lambda i, f, be, nv: (be[blk(i, nv)], 0, f)),
                  pl.BlockSpec((None, d, tf), lambda i, f, be, nv: (be[blk(i, nv)], 0, nf + f)),
                  pl.BlockSpec((None, 1, tf), lambda i, f, be, nv: (be[blk(i, nv)], 0, f)),
                  pl.BlockSpec((None, 1, tf), lambda i, f, be, nv: (be[blk(i, nv)], 0, nf + f)),
                  pl.BlockSpec((None, tf, d), lambda i, f, be, nv: (be[blk(i, nv)], f, 0)),
                  pl.BlockSpec((None, 1, d), lambda i, f, be, nv: (be[blk(i, nv)], 0, 0))],
        out_specs=pl.BlockSpec((tm, d), lambda i, f, be, nv: (blk(i, nv), 0)),
        scratch_shapes=[pltpu.VMEM((tm, d), F32)],
    )
    b_up3 = b_up.reshape(e, 1, f2)
    return pl.pallas_call(
        functools.partial(_moe_kernel, nf=nf),
        grid_spec=grid_spec,
        out_shape=jax.ShapeDtypeStruct((s, d), BF16),
        compiler_params=_params(("arbitrary", "arbitrary")),
        name="moe_experts",
    )(blk_expert, n_valid, x_sorted, w_up, w_up, b_up3, b_up3, w_down, b_down.reshape(e, 1, d))


def _combine_kernel(x_ref, *refs):
    y_refs, (gt_ref, gate2_ref, o_ref) = refs[:TOP_K], refs[TOP_K:]
    gt = gt_ref[...]
    acc = gt[:, 0:1] * y_refs[0][...].astype(F32)
    for k in range(1, TOP_K):
        acc = acc + gt[:, k:k + 1] * y_refs[k][...].astype(F32)
    o_ref[...] = x_ref[...] + gate2_ref[...] * acc


def moe_combine(x, y_gathered, gates, gate2, rows_per_mod, tm):
    t, d = x.shape
    g2op, g2_spec = _mod_operand(gate2, tm, rows_per_mod)
    row_spec = pl.BlockSpec((tm, d), lambda i: (i, 0))
    return pl.pallas_call(
        _combine_kernel,
        grid=(t // tm,),
        in_specs=[row_spec] + [row_spec] * TOP_K + [pl.BlockSpec((tm, TOP_K), lambda i: (i, 0)), g2_spec],
        out_specs=row_spec,
        out_shape=jax.ShapeDtypeStruct((t, d), F32),
        compiler_params=_params(("parallel",)),
        name="moe_combine",
    )(x, *y_gathered, gates, g2op)


def _route(logits, n_experts, tm):
    t = logits.shape[0]
    top_v, top_i = lax.top_k(logits, TOP_K)
    gates = jax.nn.softmax(top_v, axis=-1)
    tk = t * TOP_K
    flat_e = top_i.reshape(tk)
    onehot = (flat_e[:, None] == jnp.arange(n_experts, dtype=flat_e.dtype)[None, :]).astype(jnp.int32)
    csum = jnp.cumsum(onehot, axis=0)
    rank = jnp.take_along_axis(csum, flat_e[:, None], axis=1)[:, 0] - 1
    counts = csum[-1]
    padded = (counts + tm - 1) // tm * tm
    pad_end = jnp.cumsum(padded)
    pad_start = pad_end - padded
    dest = pad_start[flat_e] + rank
    n_blocks = -(-tk // tm) + n_experts
    blk_start = jnp.arange(n_blocks, dtype=jnp.int32) * tm
    blk_expert = jnp.minimum(jnp.sum(pad_end[None, :] <= blk_start[:, None], axis=1), n_experts - 1).astype(jnp.int32)
    n_valid = (pad_end[-1] // tm).astype(jnp.int32).reshape(1)
    order = jnp.argsort(flat_e, stable=True).astype(jnp.int32)
    start = jnp.cumsum(counts) - counts
    slot = jnp.arange(n_blocks * tm, dtype=jnp.int32)
    slot_e = jnp.repeat(blk_expert, tm)
    slot_rank = slot - pad_start[slot_e]
    src = jnp.clip(start[slot_e] + slot_rank, 0, tk - 1)
    slot_tok = jnp.where(slot_rank < counts[slot_e], order[src] // TOP_K, slot % t).astype(jnp.int32)
    return slot_tok, dest.reshape(t, TOP_K), gates, blk_expert, n_valid


def _pick_tile(n, pref):
    return pref if n % pref == 0 else n


def _mixers(x, mods, pos0, pool_prev16, h0_re, h0_im, kv_cache, expert_w, layer, wts, ssm_mats, lam_init):
    b, l, d = x.shape
    t = b * l
    shift1, scale1, gate1, shift2, scale2, _ = mods
    xt = x.reshape(t, d)
    tm = _pick_tile(t, 512)
    tm_in = _pick_tile(t, 1024)
    u, gates = norm_matmul(xt, wts["norm1_g"], shift1, scale1, wts["w_proj"], wts["b_proj"], wts["n_in"],
                           rows_per_mod=l, tm=tm_in)
    u3 = u.reshape(b, l, u.shape[1])
    pos = pos0 + jnp.arange(l, dtype=jnp.int32)
    tl = _pick_tile(l, 512)
    q2, kb, vb, k_rows, v_rows = qk_prep(u3, wts["q_norm_g"], wts["k_norm_g"], pos, tl)
    expert_h = None
    if kv_cache is None:
        oa, *expert_h = prompt_diff_attn(q2, kb, vb, wts["q_norm_g"], wts["k_norm_g"], wts["diff_lambda"],
                                         wts["subln_g"], expert_w[0], expert_w[1], layer, lam_init, tl)
    else:
        oa = cached_diff_attn(q2, kb, vb, kv_cache[0], kv_cache[1], layer, wts["diff_lambda"], wts["subln_g"],
                              lam_init)
    aw = ATT_HEADS * V_DIM
    pw = wts["pool_scale"].shape[0]
    pm = pool_mix(u3, (3 * aw) // pw, pool_prev16, wts["w_pool_mix"], wts["pool_scale"], pos0, tl)
    pool_new = u3[:, l - POOL_HIST:, 3 * aw:3 * aw + pw]
    ys, ssm_re, ssm_im = s5_mix(u3[:, :, 3 * aw + pw:], h0_re, h0_im, ssm_mats)
    merged = merge_branches(pm.reshape(t, pw), oa.reshape(t, aw), ys.reshape(t, -1), gates, wts, tm)
    x_new, h2, logits = out_proj(merged, xt, gate1, shift2, scale2, wts, l, tm)
    outs = (k_rows.reshape(b, l, ATT_HEADS, 2 * QK_DIM), v_rows.reshape(b, l, ATT_HEADS, V_DIM),
            pool_new, ssm_re, ssm_im)
    return x_new, h2, logits, outs, expert_h


def kernel(x_prompt, x_sample, c_prompt, c_sample, cache_k, cache_v, cache_pool, state_ssm_re, state_ssm_im, w_ada, b_ada, norm1_g, w_in, w_gate, b_gate, w_pool_mix, pool_scale, w_pool_out, q_norm_g, k_norm_g, diff_lambda, subln_g, w_attn_out, ssm_lambda_re, ssm_lambda_im, ssm_log_dt, ssm_b_re, ssm_b_im, ssm_c_re, ssm_c_im, ssm_d, w_glu, b_glu, w_ssm_out, w_o, norm2_g, w_router, b_router, w_up, b_up, w_down, b_down):
    nb, seq, d = x_prompt.shape
    db, dseq, _ = x_sample.shape
    depth = w_ada.shape[0]
    past = cache_k.shape[2]
    n_exp = w_router.shape[-1]
    aw = ATT_HEADS * V_DIM
    pw = pool_scale.shape[-1]
    tp, ts = nb * seq, db * dseq

    n_c = nb + db
    c_rows = -(-n_c // 8) * 8
    c_pad = jnp.zeros((c_rows, d), F32).at[:n_c].set(jnp.concatenate([c_prompt, c_sample], axis=0))
    mod_all = ada_mod(c_pad, w_ada, b_ada)

    cache_k4 = cache_k.reshape(depth, db, past, aw)
    cache_v4 = cache_v.reshape(depth, db, past, aw)
    pool_zero = jnp.zeros((nb, HIST_ROWS, pw), F32)
    ssm_zero = jnp.zeros((nb,) + state_ssm_re.shape[2:], F32)
    pad_hist = HIST_ROWS - POOL_HIST
    moe_tm, moe_tf = 1024, 512

    xp, xs = x_prompt, x_sample
    outs_p, outs_s = [], []
    for l in range(depth):
        lam_init = 0.8 - 0.6 * math.exp(-0.3 * l)
        wi = w_in[l]
        w_in_r = jnp.concatenate([wi[:, pw:pw + 3 * aw], wi[:, :pw], wi[:, pw + 3 * aw:]], axis=1).astype(BF16)
        wr = jnp.zeros((d, ROUTER_PAD), F32).at[:, :n_exp].set(w_router[l])
        wr_hi = wr.astype(BF16)
        wr = jnp.stack([wr_hi, (wr - wr_hi.astype(F32)).astype(BF16)])
        br = jnp.zeros((1, ROUTER_PAD), F32).at[0, :n_exp].set(b_router[l])
        wts = {
            "norm1_g": norm1_g[l], "n_in": w_in_r.shape[1],
            "w_proj": jnp.concatenate([w_in_r, w_gate[l].astype(BF16)], axis=1),
            "b_proj": jnp.concatenate([jnp.zeros((w_in_r.shape[1],), F32), b_gate[l]]),
            "w_pool_mix": w_pool_mix[l], "pool_scale": pool_scale[l],
            "q_norm_g": q_norm_g[l], "k_norm_g": k_norm_g[l], "diff_lambda": diff_lambda[l], "subln_g": subln_g[l],
            "w_glu": w_glu[l].astype(BF16), "b_glu": b_glu[l].reshape(1, -1),
            "w_pool_out": w_pool_out[l].astype(BF16), "w_attn_out": w_attn_out[l].astype(BF16),
            "w_ssm_out": w_ssm_out[l].astype(BF16), "w_o": w_o[l].astype(BF16),
            "norm2_g": norm2_g[l].reshape(1, d), "w_router": wr, "b_router": br,
        }
        ssm_mats = _ssm_matrices(ssm_lambda_re[l], ssm_lambda_im[l], ssm_log_dt[l], ssm_b_re[l], ssm_b_im[l],
                                 ssm_c_re[l], ssm_c_im[l], ssm_d[l])
        mods = jnp.split(mod_all[l], 6, axis=-1)
        mods_p = [m[:nb] for m in mods]
        mods_s = [m[nb:n_c] for m in mods]
        prev_s = jnp.pad(cache_pool[l], ((0, 0), (pad_hist, 0), (0, 0)))

        xp_mid, h2_p, lg_p, o_p, (w_up_h, w_down_h) = _mixers(
            xp, mods_p, 0, pool_zero, ssm_zero, ssm_zero, None, (w_up, w_down), l, wts, ssm_mats, lam_init)
        xs_mid, h2_s, lg_s, o_s, _ = _mixers(xs, mods_s, past, prev_s, state_ssm_re[l], state_ssm_im[l],
                                             (cache_k4, cache_v4), None, l, wts, ssm_mats, lam_init)
        outs_p.append(o_p)
        outs_s.append(o_s)

        h2_all = jnp.concatenate([h2_p, h2_s], axis=0)
        logits = jnp.concatenate([lg_p[:, :n_exp], lg_s[:, :n_exp]], axis=0)
        slot_tok, dest, gates, blk_expert, n_valid = _route(logits, n_exp, moe_tm)
        y_sorted = moe_experts(h2_all[slot_tok], blk_expert, n_valid, w_up_h, b_up[l], w_down_h, b_down[l], moe_tm,
                               moe_tf)
        yg_p = [y_sorted[dest[:tp, k]] for k in range(TOP_K)]
        yg_s = [y_sorted[dest[tp:, k]] for k in range(TOP_K)]
        xp = moe_combine(xp_mid, yg_p, gates[:tp], mods_p[5], seq, _pick_tile(tp, 512)).reshape(nb, seq, d)
        xs = moe_combine(xs_mid, yg_s, gates[tp:], mods_s[5], dseq, _pick_tile(ts, 512)).reshape(db, dseq, d)

    stack = lambda outs, i: jnp.stack([o[i] for o in outs])
    return (xp, xs,
            stack(outs_p, 0), stack(outs_p, 1), stack(outs_p, 2), stack(outs_p, 3), stack(outs_p, 4),
            stack(outs_s, 0), stack(outs_s, 1), stack(outs_s, 2), stack(outs_s, 3), stack(outs_s, 4))
```
